```python
import jax, jax.numpy as jnp
from jax import lax
import numpy as np

D_MODEL = 4096
BATCH = 4
SEQ = 2048
DEPTH = 2
DEC_BATCH = 8
DEC_SEQ = 1
PAST_LEN = 16384
PAGE_SIZE = 128

N_A_LAYERS = DEPTH // 2
N_B_LAYERS = DEPTH - N_A_LAYERS

CHUNK = 128
GMLP_DFF = 2 * D_MODEL
GMLP_GROUPS = 16
GMLP_GDIM = GMLP_DFF // GMLP_GROUPS

N_HEADS = 32
HEAD_DIM = D_MODEL // N_HEADS
N_KV_HEADS = 4
HEADS_PER_KV = N_HEADS // N_KV_HEADS
N_BRANCH = 3
CMP_BLOCK = 32
CMP_STRIDE = 16
SEL_BLOCK = 64
N_SEL = 16
WINDOW = 512
NSA_QBLK = 32

N_EXPERTS = 64
TOP_K = 8
N_GROUPS = 8
TOPK_GROUPS = 4
EXPERT_DFF = D_MODEL // 4
ROUTE_SCALE = 2.5
MOE_ROW_BLOCK = 128

RMS_EPS = 1e-6
LN_EPS = 1e-5
NEG_INF = -1e30
FORCE = 1e30

kernel_name = 'yoco_gmlp_nsa_moe_step'


def rmsnorm(x, g):
    xf = x.astype(jnp.float32)
    y = xf * lax.rsqrt(jnp.mean(xf * xf, axis=-1, keepdims=True) + RMS_EPS)
    return (y * g.astype(jnp.float32)).astype(x.dtype)


def layernorm(x, g, b):
    xf = x.astype(jnp.float32)
    xc = xf - jnp.mean(xf, axis=-1, keepdims=True)
    y = xc * lax.rsqrt(jnp.mean(xc * xc, axis=-1, keepdims=True) + LN_EPS)
    return (y * g.astype(jnp.float32) + b.astype(jnp.float32)).astype(x.dtype)


def masked_softmax(s, mask):
    maskf = mask.astype(jnp.float32)
    s = jnp.where(mask, s, NEG_INF)
    e = jnp.exp(s - jnp.max(s, axis=-1, keepdims=True)) * maskf
    d = jnp.sum(e, axis=-1, keepdims=True)
    return e / jnp.where(d > 0, d, 1.0)


def alibi_slopes():
    h = jnp.arange(1, N_HEADS + 1, dtype=jnp.float32)
    return (2.0 ** (-8.0 * h / N_HEADS)).reshape(N_KV_HEADS, HEADS_PER_KV)


def gmlp_mixer(h, w_in, b_in, ln_g, ln_b, w_s, b_s, w_out):
    B, T, _ = h.shape
    n = CHUNK if T % CHUNK == 0 else T
    uv = jax.nn.gelu(h @ w_in + b_in)
    u, v = jnp.split(uv, 2, axis=-1)
    v = layernorm(v, ln_g, ln_b)
    vg = v.reshape(B, T // n, n, GMLP_GROUPS, GMLP_GDIM)
    causal = jnp.tril(jnp.ones((n, n), dtype=bool))
    ws = jnp.where(causal, w_s[:, :n, :n], 0.0).astype(v.dtype)
    bias = b_s[:, :n].T.astype(v.dtype)[None, None, :, :, None]
    mixed = jnp.einsum('gts,bcsgd->bctgd', ws, vg) + bias
    out = (u * mixed.reshape(B, T, GMLP_DFF)) @ w_out
    return out, v


def grouped_experts(xt, eidx, gw, w1, w3, w2):
    N, D = xt.shape
    P = N * TOP_K
    R = int(min(MOE_ROW_BLOCK, max(8, P // N_EXPERTS)))
    nblk = -(-(P + N_EXPERTS * (R - 1)) // R)
    flat_e = eidx.reshape(P)
    flat_tok = jnp.repeat(jnp.arange(N, dtype=jnp.int32), TOP_K)
    flat_g = gw.reshape(P)
    order = jnp.argsort(flat_e)
    se, stok, sg = flat_e[order], flat_tok[order], flat_g[order]
    counts = jnp.bincount(flat_e, length=N_EXPERTS)
    starts = jnp.cumsum(counts) - counts
    padded = (counts + R - 1) // R * R
    pends = jnp.cumsum(padded)
    pstarts = pends - padded
    dest = pstarts[se] + (jnp.arange(P) - starts[se])
    rows = nblk * R
    row_tok = jnp.full((rows,), N, dtype=jnp.int32).at[dest].set(stok)
    row_g = jnp.zeros((rows,), xt.dtype).at[dest].set(sg)
    blk_e = jnp.clip(jnp.searchsorted(pends, jnp.arange(nblk) * R, side='right'), 0, N_EXPERTS - 1)
    x_pad = jnp.concatenate([xt, jnp.zeros((1, D), xt.dtype)], axis=0)

    def one_block(args):
        toks, e = args
        xb = x_pad[toks]
        hb = jax.nn.silu(xb @ w1[e]) * (xb @ w3[e])
        return hb @ w2[e]

    yb = lax.map(one_block, (row_tok.reshape(nblk, R), blk_e))
    y = jnp.zeros((N + 1, D), xt.dtype).at[row_tok].add(yb.reshape(rows, D) * row_g[:, None])
    return y[:N]


def moe_ffn(h, w_router, b_router, w1, w3, w2, ws1, ws3, ws2):
    B, T, D = h.shape
    N = B * T
    xt = h.reshape(N, D)
    s = jax.nn.sigmoid((xt @ w_router).astype(jnp.float32))
    sel = s + b_router.astype(jnp.float32)
    gsc = jnp.sum(lax.top_k(sel.reshape(N, N_GROUPS, N_EXPERTS // N_GROUPS), 2)[0], axis=-1)
    _, gidx = lax.top_k(gsc, TOPK_GROUPS)
    gmask = jnp.sum(jax.nn.one_hot(gidx, N_GROUPS, dtype=jnp.float32), axis=1)
    emask = jnp.repeat(gmask, N_EXPERTS // N_GROUPS, axis=1) > 0
    _, eidx = lax.top_k(jnp.where(emask, sel, NEG_INF), TOP_K)
    gw = jnp.take_along_axis(s, eidx, axis=1)
    gw = gw / jnp.sum(gw, axis=-1, keepdims=True) * ROUTE_SCALE
    routed = grouped_experts(xt, eidx, gw.astype(xt.dtype), w1, w3, w2)
    shared = (jax.nn.silu(xt @ ws1) * (xt @ ws3)) @ ws2
    return (routed + shared).reshape(B, T, D)


def compress(k, pe, w1, b1, w2):
    B, L = k.shape[:2]
    nseg = L // CMP_STRIDE
    n_sub = CMP_BLOCK // CMP_STRIDE
    nc = nseg - n_sub + 1
    seg = k[:, :nseg * CMP_STRIDE].reshape(B, nseg, CMP_STRIDE, N_KV_HEADS, HEAD_DIM)
    blocks = jnp.concatenate([seg[:, i:i + nc] for i in range(n_sub)], axis=2)
    blocks = blocks + pe.astype(k.dtype)[None, None, :, None, :]
    flat = blocks.transpose(0, 1, 3, 2, 4).reshape(B, nc, N_KV_HEADS, CMP_BLOCK * HEAD_DIM)
    return jax.nn.gelu(flat @ w1 + b1) @ w2


def kv_side(kv_rows, pe_cmp, w_cmp1, b_cmp1, w_cmp2, pos0, past):
    B, T = kv_rows.shape[:2]
    cmp_all, slc_all, win_rows = kv_rows[:, :, 0], kv_rows[:, :, 1], kv_rows[:, :, 2]
    tail = win_rows.shape[2:]
    if past is None:
        win_prev = jnp.zeros((B, WINDOW) + tail, win_rows.dtype)
    else:
        cmp_past, slc_past, win_buf = past
        cmp_all = jnp.concatenate([cmp_past.astype(cmp_all.dtype), cmp_all], axis=1)
        slc_all = jnp.concatenate([slc_past.astype(slc_all.dtype), slc_all], axis=1)
        win_prev = jnp.concatenate([jnp.zeros((B, WINDOW - win_buf.shape[1]) + tail, win_rows.dtype),
                                    win_buf.astype(win_rows.dtype)], axis=1)
    win_all = jnp.concatenate([win_prev, win_rows], axis=1)
    L = cmp_all.shape[1]
    kc = compress(cmp_all[:, :, 0], pe_cmp[0], w_cmp1[0], b_cmp1[0], w_cmp2[0])
    vc = compress(cmp_all[:, :, 1], pe_cmp[1], w_cmp1[1], b_cmp1[1], w_cmp2[1])
    nc = kc.shape[1]
    cpos = jnp.arange(nc) * CMP_STRIDE + CMP_BLOCK - 1
    nsb = -(-L // SEL_BLOCK)
    slc_pad = jnp.pad(slc_all, ((0, 0), (0, nsb * SEL_BLOCK - L), (0, 0), (0, 0), (0, 0)))
    blk = slc_pad.reshape(B, nsb, SEL_BLOCK, 2, N_KV_HEADS, HEAD_DIM).transpose(3, 0, 4, 1, 2, 5)
    ci = jnp.arange(nc)[:, None] * CMP_STRIDE
    sj = jnp.arange(nsb)[None, :] * SEL_BLOCK
    overlap = ((ci < sj + SEL_BLOCK) & (ci + CMP_BLOCK > sj)).astype(jnp.float32)
    return dict(kc=kc, vc=vc, cpos=cpos, overlap=overlap, ks_blk=blk[0], vs_blk=blk[1],
                kw=win_all[:, :, 0], vw=win_all[:, :, 1])


def nsa_attend(q, g, pos_q, kc, vc, cpos, overlap, ks_blk, vs_blk, kw, vw, wpos):
    f32 = jnp.float32
    B, Q = q.shape[:2]
    G = N_KV_HEADS
    nsb = ks_blk.shape[2]
    scale = HEAD_DIM ** -0.5
    slopes = alibi_slopes()[None, None, :, :, None]
    dc = (pos_q[:, None] - cpos[None, :]).astype(f32)
    sc = jnp.einsum('bqgrd,bcgd->bqgrc', q, kc).astype(f32) * scale - slopes * dc[None, :, None, None, :]
    pc = masked_softmax(sc, (dc >= 0)[None, :, None, None, :])
    o_c = jnp.einsum('bqgrc,bcgd->bqgrd', pc.astype(vc.dtype), vc)
    imp = jnp.einsum('bqgrc,cj->bqgj', pc, overlap)
    jt = pos_q // SEL_BLOCK
    jj = jnp.arange(nsb)
    valid = jj[None, :] <= jt[:, None]
    forced = (jj[None, :] == 0) | (jj[None, :] == jt[:, None]) | (jj[None, :] == jt[:, None] - 1)
    imp = jnp.where(forced[None, :, None, :], FORCE, jnp.where(valid[None, :, None, :], imp, -FORCE))
    _, top = lax.top_k(imp, min(N_SEL, nsb))
    top_g = top.transpose(0, 2, 1, 3)
    take = jax.vmap(jax.vmap(lambda blk, ix: blk[ix]))
    nk = top_g.shape[-1] * SEL_BLOCK
    kg = take(ks_blk, top_g).reshape(B, G, Q, nk, HEAD_DIM)
    vg = take(vs_blk, top_g).reshape(B, G, Q, nk, HEAD_DIM)
    kpos = (top_g[..., None] * SEL_BLOCK + jnp.arange(SEL_BLOCK)).reshape(B, G, Q, nk)
    ds = (pos_q[None, None, :, None] - kpos).astype(f32).transpose(0, 2, 1, 3)[:, :, :, None, :]
    ss = jnp.einsum('bqgrd,bgqnd->bqgrn', q, kg).astype(f32) * scale - slopes * ds
    ps = masked_softmax(ss, ds >= 0)
    o_s = jnp.einsum('bqgrn,bgqnd->bqgrd', ps.astype(vg.dtype), vg)
    dw = (pos_q[:, None] - wpos[None, :]).astype(f32)
    mw = (dw >= 0) & (dw <= WINDOW) & (wpos >= 0)[None, :]
    sw = jnp.einsum('bqgrd,blgd->bqgrl', q, kw).astype(f32) * scale - slopes * dw[None, :, None, None, :]
    pw = masked_softmax(sw, mw[None, :, None, None, :])
    o_w = jnp.einsum('bqgrl,blgd->bqgrd', pw.astype(vw.dtype), vw)
    o = g[..., 0:1] * o_c + g[..., 1:2] * o_s + g[..., 2:3] * o_w
    return o.reshape(B, Q, N_HEADS * HEAD_DIM).astype(q.dtype)


def nsa_mixer(hn, ctx, w_qg, w_o, pos0):
    B, T, _ = hn.shape
    HD = N_HEADS * HEAD_DIM
    qg = hn @ w_qg
    q = qg[..., :HD].reshape(B, T, N_KV_HEADS, HEADS_PER_KV, HEAD_DIM)
    gates = jax.nn.sigmoid(qg[..., HD:].astype(jnp.float32)).reshape(B, T, N_KV_HEADS, HEADS_PER_KV, N_BRANCH)
    QB = NSA_QBLK if T % NSA_QBLK == 0 else T

    def attend_block(b0):
        pos_q = pos0 + b0 + jnp.arange(QB)
        wpos = pos0 + b0 - WINDOW + jnp.arange(WINDOW + QB)
        return nsa_attend(lax.dynamic_slice_in_dim(q, b0, QB, axis=1),
                          lax.dynamic_slice_in_dim(gates, b0, QB, axis=1),
                          pos_q, ctx['kc'], ctx['vc'], ctx['cpos'], ctx['overlap'],
                          ctx['ks_blk'], ctx['vs_blk'],
                          lax.dynamic_slice_in_dim(ctx['kw'], b0, WINDOW + QB, axis=1),
                          lax.dynamic_slice_in_dim(ctx['vw'], b0, WINDOW + QB, axis=1), wpos)

    o = lax.map(attend_block, jnp.arange(T // QB) * QB)
    o = jnp.moveaxis(o, 0, 1).reshape(B, T, HD)
    return o @ w_o


def trunk(x, c, p, pos0, past):
    B, T, _ = x.shape
    h = x
    c_act = jax.nn.silu(c)
    v_rows, kv_rows, ctx = [], None, None
    for l in range(DEPTH):
        mod = (c_act @ p['w_ada'][l] + p['b_ada'][l])[:, None, :]
        sh1, sc1, gt1, sh2, sc2, gt2 = jnp.split(mod, 6, axis=-1)
        hn = rmsnorm(h, p['g_norm_mix'][l]) * (1 + sc1) + sh1
        if l < N_A_LAYERS:
            mix, v = gmlp_mixer(hn, p['w_gmlp_in'][l], p['b_gmlp_in'][l], p['g_sgu_ln'][l], p['b_sgu_ln'][l],
                                p['w_sgu'][l], p['b_sgu'][l], p['w_gmlp_out'][l])
            v_rows.append(v[:, T - min(T, CHUNK):])
        else:
            if ctx is None:
                mkv = (c_act @ p['w_ada_kv'] + p['b_ada_kv'])[:, None, :]
                sh_kv, sc_kv = jnp.split(mkv, 2, axis=-1)
                kvn = rmsnorm(h, p['g_norm_kv']) * (1 + sc_kv) + sh_kv
                kv_rows = (kvn @ p['w_kv']).reshape(B, T, N_BRANCH, 2, N_KV_HEADS, HEAD_DIM)
                ctx = kv_side(kv_rows, p['pe_cmp'], p['w_cmp1'], p['b_cmp1'], p['w_cmp2'], pos0, past)
            j = l - N_A_LAYERS
            mix = nsa_mixer(hn, ctx, p['w_qg'][j], p['w_o'][j], pos0)
        h = h + gt1 * mix
        hn = rmsnorm(h, p['g_norm_ffn'][l]) * (1 + sc2) + sh2
        h = h + gt2 * moe_ffn(hn, p['w_router'][l], p['b_router'][l], p['w_exp1'][l], p['w_exp3'][l],
                              p['w_exp2'][l], p['w_sh1'][l], p['w_sh3'][l], p['w_sh2'][l])
    return rmsnorm(h, p['g_final']), jnp.stack(v_rows), kv_rows


def setup_inputs(seed: int = 0) -> dict:
    key = jax.random.key(seed)
    keys = list(jax.random.split(key, 48))
    f32 = jnp.float32

    def nrm(shape, scale=1.0):
        return jax.random.normal(keys.pop(), shape, f32) * scale

    D = D_MODEL
    HD = N_HEADS * HEAD_DIM
    KVW = N_BRANCH * 2 * N_KV_HEADS * HEAD_DIM
    n_pages = PAST_LEN // PAGE_SIZE
    n_used = DEC_BATCH * n_pages
    n_pool = n_used + max(1, n_used // 4)
    page_table = jax.random.permutation(keys.pop(), n_pool)[:n_used].reshape(DEC_BATCH, n_pages).astype(jnp.int32)
    win_rows = min(WINDOW, PAST_LEN)
    return {
        'x_prompt': nrm((BATCH, SEQ, D)),
        'x_sample': nrm((DEC_BATCH, DEC_SEQ, D)),
        'c_prompt': nrm((BATCH, D)),
        'c_sample': nrm((DEC_BATCH, D)),
        'cache_cmp_kv': nrm((n_pool, PAGE_SIZE, 2, N_KV_HEADS, HEAD_DIM)),
        'cache_slc_kv': nrm((n_pool, PAGE_SIZE, 2, N_KV_HEADS, HEAD_DIM)),
        'state_win_kv': nrm((DEC_BATCH, win_rows, 2, N_KV_HEADS, HEAD_DIM)),
        'page_table': page_table,
        'w_ada': nrm((DEPTH, D, 6 * D), 0.5 * D ** -0.5),
        'b_ada': nrm((DEPTH, 6 * D), 0.01),
        'g_norm_mix': 1.0 + nrm((DEPTH, D), 0.02),
        'g_norm_ffn': 1.0 + nrm((DEPTH, D), 0.02),
        'w_gmlp_in': nrm((N_A_LAYERS, D, 2 * GMLP_DFF), D ** -0.5),
        'b_gmlp_in': nrm((N_A_LAYERS, 2 * GMLP_DFF), 0.01),
        'g_sgu_ln': 1.0 + nrm((N_A_LAYERS, GMLP_DFF), 0.02),
        'b_sgu_ln': nrm((N_A_LAYERS, GMLP_DFF), 0.01),
        'w_sgu': nrm((N_A_LAYERS, GMLP_GROUPS, CHUNK, CHUNK), CHUNK ** -0.5),
        'b_sgu': nrm((N_A_LAYERS, GMLP_GROUPS, CHUNK), 0.01),
        'w_gmlp_out': nrm((N_A_LAYERS, GMLP_DFF, D), GMLP_DFF ** -0.5),
        'w_ada_kv': nrm((D, 2 * D), 0.5 * D ** -0.5),
        'b_ada_kv': nrm((2 * D,), 0.01),
        'g_norm_kv': 1.0 + nrm((D,), 0.02),
        'w_kv': nrm((D, KVW), D ** -0.5),
        'pe_cmp': nrm((2, CMP_BLOCK, HEAD_DIM), 0.1),
        'w_cmp1': nrm((2, CMP_BLOCK * HEAD_DIM, HEAD_DIM), (CMP_BLOCK * HEAD_DIM) ** -0.5),
        'b_cmp1': nrm((2, HEAD_DIM), 0.01),
        'w_cmp2': nrm((2, HEAD_DIM, HEAD_DIM), HEAD_DIM ** -0.5),
        'w_qg': nrm((N_B_LAYERS, D, HD + N_BRANCH * N_HEADS), D ** -0.5),
        'w_o': nrm((N_B_LAYERS, HD, D), HD ** -0.5),
        'w_router': nrm((DEPTH, D, N_EXPERTS), D ** -0.5),
        'b_router': nrm((DEPTH, N_EXPERTS), 0.01),
        'w_exp1': nrm((DEPTH, N_EXPERTS, D, EXPERT_DFF), D ** -0.5),
        'w_exp3': nrm((DEPTH, N_EXPERTS, D, EXPERT_DFF), D ** -0.5),
        'w_exp2': nrm((DEPTH, N_EXPERTS, EXPERT_DFF, D), EXPERT_DFF ** -0.5),
        'w_sh1': nrm((DEPTH, D, EXPERT_DFF), D ** -0.5),
        'w_sh3': nrm((DEPTH, D, EXPERT_DFF), D ** -0.5),
        'w_sh2': nrm((DEPTH, EXPERT_DFF, D), EXPERT_DFF ** -0.5),
        'g_final': 1.0 + nrm((D,), 0.02),
    }


def reference(x_prompt, x_sample, c_prompt, c_sample, cache_cmp_kv, cache_slc_kv, state_win_kv, page_table,
              w_ada, b_ada, g_norm_mix, g_norm_ffn, w_gmlp_in, b_gmlp_in, g_sgu_ln, b_sgu_ln, w_sgu, b_sgu,
              w_gmlp_out, w_ada_kv, b_ada_kv, g_norm_kv, w_kv, pe_cmp, w_cmp1, b_cmp1, w_cmp2, w_qg, w_o,
              w_router, b_router, w_exp1, w_exp3, w_exp2, w_sh1, w_sh3, w_sh2, g_final):
    p = dict(w_ada=w_ada, b_ada=b_ada, g_norm_mix=g_norm_mix, g_norm_ffn=g_norm_ffn,
             w_gmlp_in=w_gmlp_in, b_gmlp_in=b_gmlp_in, g_sgu_ln=g_sgu_ln, b_sgu_ln=b_sgu_ln,
             w_sgu=w_sgu, b_sgu=b_sgu, w_gmlp_out=w_gmlp_out, w_ada_kv=w_ada_kv, b_ada_kv=b_ada_kv,
             g_norm_kv=g_norm_kv, w_kv=w_kv, pe_cmp=pe_cmp, w_cmp1=w_cmp1, b_cmp1=b_cmp1, w_cmp2=w_cmp2,
             w_qg=w_qg, w_o=w_o, w_router=w_router, b_router=b_router, w_exp1=w_exp1, w_exp3=w_exp3,
             w_exp2=w_exp2, w_sh1=w_sh1, w_sh3=w_sh3, w_sh2=w_sh2, g_final=g_final)
    y_prompt, v_prompt, kv_p = trunk(x_prompt, c_prompt, p, 0, None)
    n_seq, n_pages = page_table.shape
    page = cache_cmp_kv.shape[1]
    past_len = n_pages * page
    cmp_past = cache_cmp_kv[page_table].reshape(n_seq, past_len, 2, N_KV_HEADS, HEAD_DIM)
    slc_past = cache_slc_kv[page_table].reshape(n_seq, past_len, 2, N_KV_HEADS, HEAD_DIM)
    y_sample, v_sample, kv_s = trunk(x_sample, c_sample, p, past_len, (cmp_past, slc_past, state_win_kv))
    B, T = x_prompt.shape[:2]
    cmp_kv_prompt = kv_p[:, :, 0].reshape(B, T // page, page, 2, N_KV_HEADS, HEAD_DIM)
    slc_kv_prompt = kv_p[:, :, 1].reshape(B, T // page, page, 2, N_KV_HEADS, HEAD_DIM)
    win_kv_prompt = kv_p[:, T - min(WINDOW, T):, 2]
    return (y_prompt, y_sample, v_prompt, v_sample, cmp_kv_prompt, kv_s[:, :, 0], slc_kv_prompt, kv_s[:, :, 1], win_kv_prompt, kv_s[:, :, 2])
```

```python
import functools

import jax
import jax.numpy as jnp
from jax import lax
from jax.experimental import pallas as pl
from jax.experimental.pallas import tpu as pltpu

D_MODEL = 4096
DEPTH = 2
PAGE_SIZE = 128
N_A_LAYERS = DEPTH // 2
CHUNK = 128
GMLP_DFF = 2 * D_MODEL
GMLP_GROUPS = 16
GMLP_GDIM = GMLP_DFF // GMLP_GROUPS
N_HEADS = 32
HEAD_DIM = D_MODEL // N_HEADS
N_KV_HEADS = 4
HEADS_PER_KV = N_HEADS // N_KV_HEADS
N_BRANCH = 3
CMP_BLOCK = 32
CMP_STRIDE = 16
SEL_BLOCK = 64
N_SEL = 16
WINDOW = 512
NSA_QBLK = 32
N_EXPERTS = 64
TOP_K = 8
N_GROUPS = 8
TOPK_GROUPS = 4
EXPERT_DFF = D_MODEL // 4
ROUTE_SCALE = 2.5
RMS_EPS = 1e-6
LN_EPS = 1e-5
NEG_INF = -1e30
FORCE = 1e30

BF16 = jnp.bfloat16
F32 = jnp.float32

VMEM_LIMIT_BYTES = 56 * 1024 * 1024
BF16_SUBLANES = 16
MOE_FF_CHUNK = 256
MOE_OUT_CHUNK = 1024


def _mm_kernel(x_ref, w_ref, b_ref, o_ref, *, act):
    acc = jnp.dot(x_ref[...], w_ref[0].astype(BF16), preferred_element_type=F32)
    acc = acc + b_ref[...]
    if act == "gelu":
        acc = jax.nn.gelu(acc)
    o_ref[...] = acc


def _mm(x, w, layer=0, bias=None, act=None):
    M, K = x.shape
    N = w.shape[2]
    n_pad = -N % 128
    if n_pad:
        w = jnp.pad(w[layer:layer + 1], ((0, 0), (0, 0), (0, n_pad)))
        layer = 0
        if bias is not None:
            bias = jnp.pad(bias, (0, n_pad))
    Np = N + n_pad
    if M >= 1024:
        tm = 1024 if K <= 4096 else 512
    else:
        tm = M + (-M % BF16_SUBLANES)
    m_pad = -M % tm
    xb = jnp.pad(x, ((0, m_pad), (0, 0))).astype(BF16)
    Mp = M + m_pad
    if Mp // tm > 1:
        w = w[layer:layer + 1].astype(BF16)
        layer = 0
    tn = 128
    for cand in (512, 256):
        if Np % cand == 0:
            tn = cand
            break
    b2 = jnp.zeros((1, Np), F32) if bias is None else bias.reshape(1, Np).astype(F32)
    out = pl.pallas_call(
        functools.partial(_mm_kernel, act=act),
        grid=(Mp // tm, Np // tn),
        in_specs=[
            pl.BlockSpec((tm, K), lambda i, j: (i, 0)),
            pl.BlockSpec((1, K, tn), lambda i, j: (layer, 0, j)),
            pl.BlockSpec((1, tn), lambda i, j: (0, j)),
        ],
        out_specs=pl.BlockSpec((tm, tn), lambda i, j: (i, j)),
        out_shape=jax.ShapeDtypeStruct((Mp, Np), F32),
        compiler_params=pltpu.CompilerParams(
            dimension_semantics=("parallel", "arbitrary"),
            vmem_limit_bytes=VMEM_LIMIT_BYTES),
        name="dense_mm",
    )(xb, w, b2)
    return out[:M, :N]


def _is_new_expert(blk_e_ref, i):
    prev = blk_e_ref[jnp.maximum(i - 1, 0)]
    return jnp.logical_or(i == 0, blk_e_ref[i] != prev)


def _moe_up_kernel(blk_e_ref, nused_ref, x_ref, w1_ref, w3_ref, h_ref, wb_ref):
    i = pl.program_id(1)
    valid = i < nused_ref[0]

    @pl.when(valid)
    def _():
        @pl.when(_is_new_expert(blk_e_ref, i))
        def _():
            wb_ref[0] = w1_ref[0, 0].astype(BF16)
            wb_ref[1] = w3_ref[0, 0].astype(BF16)

        x = x_ref[...]
        a = jnp.dot(x, wb_ref[0], preferred_element_type=F32)
        g = jnp.dot(x, wb_ref[1], preferred_element_type=F32)
        h_ref[...] = (a * jax.nn.sigmoid(a) * g).astype(h_ref.dtype)

    @pl.when(jnp.logical_not(valid))
    def _():
        h_ref[...] = jnp.zeros_like(h_ref)


def _moe_down_kernel(blk_e_ref, nused_ref, h_ref, g_ref, w2_ref, y_ref, wb_ref):
    i = pl.program_id(1)
    valid = i < nused_ref[0]

    @pl.when(valid)
    def _():
        @pl.when(_is_new_expert(blk_e_ref, i))
        def _():
            wb_ref[...] = w2_ref[0, 0].astype(BF16)

        y = jnp.dot(h_ref[...], wb_ref[...], preferred_element_type=F32)
        y_ref[...] = y * g_ref[...]

    @pl.when(jnp.logical_not(valid))
    def _():
        y_ref[...] = jnp.zeros_like(y_ref)


def _grouped_ffn(xs, row_g, blk_e, nused, w1, w3, w2, layer, rows_per_block):
    P, D = xs.shape
    F = w1.shape[3]
    R = rows_per_block
    nb = P // R
    fc = MOE_FF_CHUNK
    oc = MOE_OUT_CHUNK

    def row_map(j, i, blk_e_ref, nused_ref):
        return (jnp.minimum(i, nused_ref[0] - 1), 0)

    def expert_of(i, blk_e_ref, nused_ref):
        return blk_e_ref[jnp.minimum(i, nused_ref[0] - 1)]

    h = pl.pallas_call(
        _moe_up_kernel,
        grid_spec=pltpu.PrefetchScalarGridSpec(
            num_scalar_prefetch=2,
            grid=(F // fc, nb),
            in_specs=[
                pl.BlockSpec((R, D), row_map),
                pl.BlockSpec((1, 1, D, fc), lambda j, i, be, nu: (layer, expert_of(i, be, nu), 0, j)),
                pl.BlockSpec((1, 1, D, fc), lambda j, i, be, nu: (layer, expert_of(i, be, nu), 0, j)),
            ],
            out_specs=pl.BlockSpec((R, fc), lambda j, i, be, nu: (i, j)),
            scratch_shapes=[pltpu.VMEM((2, D, fc), BF16)],
        ),
        out_shape=jax.ShapeDtypeStruct((P, F), BF16),
        compiler_params=pltpu.CompilerParams(
            dimension_semantics=("arbitrary", "arbitrary"),
            vmem_limit_bytes=VMEM_LIMIT_BYTES),
        name="moe_up",
    )(blk_e, nused, xs, w1, w3)

    y = pl.pallas_call(
        _moe_down_kernel,
        grid_spec=pltpu.PrefetchScalarGridSpec(
            num_scalar_prefetch=2,
            grid=(D // oc, nb),
            in_specs=[
                pl.BlockSpec((R, F), row_map),
                pl.BlockSpec((R, 1), row_map),
                pl.BlockSpec((1, 1, F, oc), lambda n, i, be, nu: (layer, expert_of(i, be, nu), 0, n)),
            ],
            out_specs=pl.BlockSpec((R, oc), lambda n, i, be, nu: (i, n)),
            scratch_shapes=[pltpu.VMEM((F, oc), BF16)],
        ),
        out_shape=jax.ShapeDtypeStruct((P, D), F32),
        compiler_params=pltpu.CompilerParams(
            dimension_semantics=("arbitrary", "arbitrary"),
            vmem_limit_bytes=VMEM_LIMIT_BYTES),
        name="moe_down",
    )(blk_e, nused, h, row_g, w2)
    return y


def _routed_experts(xt, eidx, gw, w1, w3, w2, layer):
    N, D = xt.shape
    P = N * TOP_K
    R = 256 if N >= 256 else BF16_SUBLANES
    rows = -(-(P + N_EXPERTS * (R - 1)) // R) * R
    nblk = rows // R
    flat_e = eidx.reshape(P).astype(jnp.int32)
    order = jnp.argsort(flat_e)
    se = flat_e[order]
    counts = jnp.sum(jax.nn.one_hot(flat_e, N_EXPERTS, dtype=jnp.int32), axis=0)
    starts = jnp.cumsum(counts) - counts
    padded = (counts + R - 1) // R * R
    pends = jnp.cumsum(padded)
    pstarts = pends - padded
    dest_sorted = pstarts[se] + (jnp.arange(P, dtype=jnp.int32) - starts[se])
    row_tok = jnp.full((rows,), N, dtype=jnp.int32).at[dest_sorted].set((order // TOP_K).astype(jnp.int32))
    dest = jnp.zeros((P,), jnp.int32).at[order].set(dest_sorted)
    row_g = jnp.zeros((rows,), F32).at[dest].set(gw.reshape(P))
    blk_e = jnp.clip(jnp.searchsorted(pends, jnp.arange(nblk, dtype=jnp.int32) * R, side='right'),
                     0, N_EXPERTS - 1).astype(jnp.int32)
    nused = (pends[-1] // R).astype(jnp.int32).reshape(1)
    x_pad = jnp.concatenate([xt.astype(BF16), jnp.zeros((1, D), BF16)], axis=0)
    xs = x_pad[row_tok]
    yb = _grouped_ffn(xs, row_g.reshape(rows, 1), blk_e, nused, w1, w3, w2, layer, R)
    return jnp.sum(yb[dest].reshape(N, TOP_K, D), axis=1)


def _shared_expert(xt, ws1, ws3, ws2, layer):
    N, D = xt.shape
    m_pad = -N % BF16_SUBLANES
    rows = N + m_pad
    R = 256 if rows % 256 == 0 else rows
    xs = jnp.pad(xt, ((0, m_pad), (0, 0))).astype(BF16)
    nblk = rows // R
    y = _grouped_ffn(xs, jnp.ones((rows, 1), F32), jnp.zeros((nblk,), jnp.int32),
                     jnp.full((1,), nblk, jnp.int32),
                     ws1[:, None], ws3[:, None], ws2[:, None], layer, R)
    return y[:N]


def _moe_ffn(h, p, layer):
    B, T, D = h.shape
    N = B * T
    xt = h.reshape(N, D)
    s = jax.nn.sigmoid(_mm(xt, p['w_router'], layer))
    sel = s + p['b_router'][layer].astype(F32)
    gsc = jnp.sum(lax.top_k(sel.reshape(N, N_GROUPS, N_EXPERTS // N_GROUPS), 2)[0], axis=-1)
    _, gidx = lax.top_k(gsc, TOPK_GROUPS)
    gmask = jnp.sum(jax.nn.one_hot(gidx, N_GROUPS, dtype=F32), axis=1)
    emask = jnp.repeat(gmask, N_EXPERTS // N_GROUPS, axis=1) > 0
    _, eidx = lax.top_k(jnp.where(emask, sel, NEG_INF), TOP_K)
    gw = jnp.take_along_axis(s, eidx, axis=1)
    gw = gw / jnp.sum(gw, axis=-1, keepdims=True) * ROUTE_SCALE
    routed = _routed_experts(xt, eidx, gw, p['w_exp1'], p['w_exp3'], p['w_exp2'], layer)
    shared = _shared_expert(xt, p['w_sh1'], p['w_sh3'], p['w_sh2'], layer)
    return (routed + shared).reshape(B, T, D)


def _rmsnorm(x, g):
    y = x * lax.rsqrt(jnp.mean(x * x, axis=-1, keepdims=True) + RMS_EPS)
    return y * g


def _layernorm(x, g, b):
    xc = x - jnp.mean(x, axis=-1, keepdims=True)
    y = xc * lax.rsqrt(jnp.mean(xc * xc, axis=-1, keepdims=True) + LN_EPS)
    return y * g + b


def _masked_softmax(s, mask):
    maskf = mask.astype(F32)
    s = jnp.where(mask, s, NEG_INF)
    e = jnp.exp(s - jnp.max(s, axis=-1, keepdims=True)) * maskf
    d = jnp.sum(e, axis=-1, keepdims=True)
    return e / jnp.where(d > 0, d, 1.0)


def _alibi_slopes():
    h = jnp.arange(1, N_HEADS + 1, dtype=F32)
    return (2.0 ** (-8.0 * h / N_HEADS)).reshape(N_KV_HEADS, HEADS_PER_KV)


def _gmlp_mixer(h, p, l):
    B, T, D = h.shape
    n = CHUNK if T % CHUNK == 0 else T
    uv = _mm(h.reshape(B * T, D), p['w_gmlp_in'], l, bias=p['b_gmlp_in'][l], act="gelu").reshape(B, T, 2 * GMLP_DFF)
    u, v = jnp.split(uv, 2, axis=-1)
    v = _layernorm(v, p['g_sgu_ln'][l], p['b_sgu_ln'][l])
    vg = v.reshape(B, T // n, n, GMLP_GROUPS, GMLP_GDIM)
    causal = jnp.tril(jnp.ones((n, n), dtype=bool))
    ws = jnp.where(causal, p['w_sgu'][l][:, :n, :n], 0.0)
    bias = p['b_sgu'][l][:, :n].T[None, None, :, :, None]
    mixed = jnp.einsum('gts,bcsgd->bctgd', ws, vg) + bias
    out = _mm((u * mixed.reshape(B, T, GMLP_DFF)).reshape(B * T, GMLP_DFF), p['w_gmlp_out'], l)
    return out.reshape(B, T, D), v


def _compress(k, pe, w1, b1, w2):
    B, L = k.shape[:2]
    nseg = L // CMP_STRIDE
    n_sub = CMP_BLOCK // CMP_STRIDE
    nc = nseg - n_sub + 1
    seg = k[:, :nseg * CMP_STRIDE].reshape(B, nseg, CMP_STRIDE, N_KV_HEADS, HEAD_DIM)
    blocks = jnp.concatenate([seg[:, i:i + nc] for i in range(n_sub)], axis=2)
    blocks = blocks + pe[None, None, :, None, :]
    flat = blocks.transpose(0, 1, 3, 2, 4).reshape(B * nc * N_KV_HEADS, CMP_BLOCK * HEAD_DIM)
    hid = _mm(flat, w1[None], 0, bias=b1, act="gelu")
    return _mm(hid, w2[None], 0).reshape(B, nc, N_KV_HEADS, HEAD_DIM)


def _kv_side(kv_rows, p, past):
    B, T = kv_rows.shape[:2]
    cmp_all, slc_all, win_rows = kv_rows[:, :, 0], kv_rows[:, :, 1], kv_rows[:, :, 2]
    tail = win_rows.shape[2:]
    if past is None:
        win_prev = jnp.zeros((B, WINDOW) + tail, win_rows.dtype)
    else:
        cmp_past, slc_past, win_buf = past
        cmp_all = jnp.concatenate([cmp_past, cmp_all], axis=1)
        slc_all = jnp.concatenate([slc_past, slc_all], axis=1)
        win_prev = jnp.concatenate([jnp.zeros((B, WINDOW - win_buf.shape[1]) + tail, win_rows.dtype), win_buf], axis=1)
    win_all = jnp.concatenate([win_prev, win_rows], axis=1)
    L = cmp_all.shape[1]
    kc = _compress(cmp_all[:, :, 0], p['pe_cmp'][0], p['w_cmp1'][0], p['b_cmp1'][0], p['w_cmp2'][0])
    vc = _compress(cmp_all[:, :, 1], p['pe_cmp'][1], p['w_cmp1'][1], p['b_cmp1'][1], p['w_cmp2'][1])
    nc = kc.shape[1]
    cpos = jnp.arange(nc) * CMP_STRIDE + CMP_BLOCK - 1
    nsb = -(-L // SEL_BLOCK)
    slc_pad = jnp.pad(slc_all, ((0, 0), (0, nsb * SEL_BLOCK - L), (0, 0), (0, 0), (0, 0)))
    blk = slc_pad.reshape(B, nsb, SEL_BLOCK, 2, N_KV_HEADS, HEAD_DIM).transpose(3, 0, 4, 1, 2, 5)
    ci = jnp.arange(nc)[:, None] * CMP_STRIDE
    sj = jnp.arange(nsb)[None, :] * SEL_BLOCK
    overlap = ((ci < sj + SEL_BLOCK) & (ci + CMP_BLOCK > sj)).astype(F32)
    return dict(kc=kc, vc=vc, cpos=cpos, overlap=overlap, ks_blk=blk[0], vs_blk=blk[1],
                kw=win_all[:, :, 0], vw=win_all[:, :, 1])


def _nsa_attend(q, g, pos_q, kc, vc, cpos, overlap, ks_blk, vs_blk, kw, vw, wpos):
    B, Q = q.shape[:2]
    G = N_KV_HEADS
    nsb = ks_blk.shape[2]
    scale = HEAD_DIM ** -0.5
    slopes = _alibi_slopes()[None, None, :, :, None]
    dc = (pos_q[:, None] - cpos[None, :]).astype(F32)
    sc = jnp.einsum('bqgrd,bcgd->bqgrc', q, kc) * scale - slopes * dc[None, :, None, None, :]
    pc = _masked_softmax(sc, (dc >= 0)[None, :, None, None, :])
    o_c = jnp.einsum('bqgrc,bcgd->bqgrd', pc, vc)
    imp = jnp.einsum('bqgrc,cj->bqgj', pc, overlap)
    jt = pos_q // SEL_BLOCK
    jj = jnp.arange(nsb)
    valid = jj[None, :] <= jt[:, None]
    forced = (jj[None, :] == 0) | (jj[None, :] == jt[:, None]) | (jj[None, :] == jt[:, None] - 1)
    imp = jnp.where(forced[None, :, None, :], FORCE, jnp.where(valid[None, :, None, :], imp, -FORCE))
    _, top = lax.top_k(imp, min(N_SEL, nsb))
    top_g = top.transpose(0, 2, 1, 3)
    take = jax.vmap(jax.vmap(lambda blk, ix: blk[ix]))
    nk = top_g.shape[-1] * SEL_BLOCK
    kg = take(ks_blk, top_g).reshape(B, G, Q, nk, HEAD_DIM)
    vg = take(vs_blk, top_g).reshape(B, G, Q, nk, HEAD_DIM)
    kpos = (top_g[..., None] * SEL_BLOCK + jnp.arange(SEL_BLOCK)).reshape(B, G, Q, nk)
    ds = (pos_q[None, None, :, None] - kpos).astype(F32).transpose(0, 2, 1, 3)[:, :, :, None, :]
    ss = jnp.einsum('bqgrd,bgqnd->bqgrn', q, kg) * scale - slopes * ds
    ps = _masked_softmax(ss, ds >= 0)
    o_s = jnp.einsum('bqgrn,bgqnd->bqgrd', ps, vg)
    dw = (pos_q[:, None] - wpos[None, :]).astype(F32)
    mw = (dw >= 0) & (dw <= WINDOW) & (wpos >= 0)[None, :]
    sw = jnp.einsum('bqgrd,blgd->bqgrl', q, kw) * scale - slopes * dw[None, :, None, None, :]
    pw = _masked_softmax(sw, mw[None, :, None, None, :])
    o_w = jnp.einsum('bqgrl,blgd->bqgrd', pw, vw)
    o = g[..., 0:1] * o_c + g[..., 1:2] * o_s + g[..., 2:3] * o_w
    return o.reshape(B, Q, N_HEADS * HEAD_DIM)


def _nsa_mixer(hn, ctx, p, j, pos0):
    B, T, D = hn.shape
    HD = N_HEADS * HEAD_DIM
    qg = _mm(hn.reshape(B * T, D), p['w_qg'], j).reshape(B, T, HD + N_BRANCH * N_HEADS)
    q = qg[..., :HD].reshape(B, T, N_KV_HEADS, HEADS_PER_KV, HEAD_DIM)
    gates = jax.nn.sigmoid(qg[..., HD:]).reshape(B, T, N_KV_HEADS, HEADS_PER_KV, N_BRANCH)
    QB = NSA_QBLK if T % NSA_QBLK == 0 else T

    def attend_block(b0):
        pos_q = pos0 + b0 + jnp.arange(QB)
        wpos = pos0 + b0 - WINDOW + jnp.arange(WINDOW + QB)
        return _nsa_attend(lax.dynamic_slice_in_dim(q, b0, QB, axis=1),
                           lax.dynamic_slice_in_dim(gates, b0, QB, axis=1),
                           pos_q, ctx['kc'], ctx['vc'], ctx['cpos'], ctx['overlap'],
                           ctx['ks_blk'], ctx['vs_blk'],
                           lax.dynamic_slice_in_dim(ctx['kw'], b0, WINDOW + QB, axis=1),
                           lax.dynamic_slice_in_dim(ctx['vw'], b0, WINDOW + QB, axis=1), wpos)

    o = lax.map(attend_block, jnp.arange(T // QB) * QB)
    o = jnp.moveaxis(o, 0, 1).reshape(B * T, HD)
    return _mm(o, p['w_o'], j).reshape(B, T, D)


def _trunk(x, c, p, pos0, past):
    B, T, D = x.shape
    h = x
    c_act = jax.nn.silu(c)
    v_rows, kv_rows, ctx = [], None, None
    for l in range(DEPTH):
        mod = _mm(c_act, p['w_ada'], l, bias=p['b_ada'][l])[:, None, :]
        sh1, sc1, gt1, sh2, sc2, gt2 = jnp.split(mod, 6, axis=-1)
        hn = _rmsnorm(h, p['g_norm_mix'][l]) * (1 + sc1) + sh1
        if l < N_A_LAYERS:
            mix, v = _gmlp_mixer(hn, p, l)
            v_rows.append(v[:, T - min(T, CHUNK):])
        else:
            if ctx is None:
                mkv = _mm(c_act, p['w_ada_kv'][None], 0, bias=p['b_ada_kv'])[:, None, :]
                sh_kv, sc_kv = jnp.split(mkv, 2, axis=-1)
                kvn = _rmsnorm(h, p['g_norm_kv']) * (1 + sc_kv) + sh_kv
                kv_rows = _mm(kvn.reshape(B * T, D), p['w_kv'][None], 0).reshape(
                    B, T, N_BRANCH, 2, N_KV_HEADS, HEAD_DIM)
                ctx = _kv_side(kv_rows, p, past)
            mix = _nsa_mixer(hn, ctx, p, l - N_A_LAYERS, pos0)
        h = h + gt1 * mix
        hn = _rmsnorm(h, p['g_norm_ffn'][l]) * (1 + sc2) + sh2
        h = h + gt2 * _moe_ffn(hn, p, l)
    return _rmsnorm(h, p['g_final']), jnp.stack(v_rows), kv_rows


def kernel(x_prompt, x_sample, c_prompt, c_sample, cache_cmp_kv, cache_slc_kv, state_win_kv, page_table,
           w_ada, b_ada, g_norm_mix, g_norm_ffn, w_gmlp_in, b_gmlp_in, g_sgu_ln, b_sgu_ln, w_sgu, b_sgu,
           w_gmlp_out, w_ada_kv, b_ada_kv, g_norm_kv, w_kv, pe_cmp, w_cmp1, b_cmp1, w_cmp2, w_qg, w_o,
           w_router, b_router, w_exp1, w_exp3, w_exp2, w_sh1, w_sh3, w_sh2, g_final):
    p = dict(w_ada=w_ada, b_ada=b_ada, g_norm_mix=g_norm_mix, g_norm_ffn=g_norm_ffn,
             w_gmlp_in=w_gmlp_in, b_gmlp_in=b_gmlp_in, g_sgu_ln=g_sgu_ln, b_sgu_ln=b_sgu_ln,
             w_sgu=w_sgu, b_sgu=b_sgu, w_gmlp_out=w_gmlp_out, w_ada_kv=w_ada_kv, b_ada_kv=b_ada_kv,
             g_norm_kv=g_norm_kv, w_kv=w_kv, pe_cmp=pe_cmp, w_cmp1=w_cmp1, b_cmp1=b_cmp1, w_cmp2=w_cmp2,
             w_qg=w_qg, w_o=w_o, w_router=w_router, b_router=b_router, w_exp1=w_exp1, w_exp3=w_exp3,
             w_exp2=w_exp2, w_sh1=w_sh1, w_sh3=w_sh3, w_sh2=w_sh2, g_final=g_final)
    y_prompt, v_prompt, kv_p = _trunk(x_prompt, c_prompt, p, 0, None)
    n_seq, n_pages = page_table.shape
    page = cache_cmp_kv.shape[1]
    past_len = n_pages * page
    cmp_past = cache_cmp_kv[page_table].reshape(n_seq, past_len, 2, N_KV_HEADS, HEAD_DIM)
    slc_past = cache_slc_kv[page_table].reshape(n_seq, past_len, 2, N_KV_HEADS, HEAD_DIM)
    y_sample, v_sample, kv_s = _trunk(x_sample, c_sample, p, past_len, (cmp_past, slc_past, state_win_kv))
    B, T = x_prompt.shape[:2]
    cmp_kv_prompt = kv_p[:, :, 0].reshape(B, T // page, page, 2, N_KV_HEADS, HEAD_DIM)
    slc_kv_prompt = kv_p[:, :, 1].reshape(B, T // page, page, 2, N_KV_HEADS, HEAD_DIM)
    win_kv_prompt = kv_p[:, T - min(WINDOW, T):, 2]
    return (y_prompt, y_sample, v_prompt, v_sample, cmp_kv_prompt, kv_s[:, :, 0], slc_kv_prompt,
            kv_s[:, :, 1], win_kv_prompt, kv_s[:, :, 2])
```

```python
import functools

import jax
import jax.numpy as jnp
from jax import lax
from jax.experimental import pallas as pl
from jax.experimental.pallas import tpu as pltpu

D_MODEL = 4096
DEPTH = 2
PAGE_SIZE = 128
N_A_LAYERS = DEPTH // 2
CHUNK = 128
GMLP_DFF = 2 * D_MODEL
GMLP_GROUPS = 16
GMLP_GDIM = GMLP_DFF // GMLP_GROUPS
N_HEADS = 32
HEAD_DIM = D_MODEL // N_HEADS
N_KV_HEADS = 4
HEADS_PER_KV = N_HEADS // N_KV_HEADS
N_BRANCH = 3
CMP_BLOCK = 32
CMP_STRIDE = 16
SEL_BLOCK = 64
N_SEL = 16
WINDOW = 512
NSA_QBLK = 32
N_EXPERTS = 64
TOP_K = 8
N_GROUPS = 8
TOPK_GROUPS = 4
EXPERT_DFF = D_MODEL // 4
ROUTE_SCALE = 2.5
RMS_EPS = 1e-6
LN_EPS = 1e-5
NEG_INF = -1e30
FORCE = 1e30

BF16 = jnp.bfloat16
F32 = jnp.float32

VMEM_LIMIT_BYTES = 56 * 1024 * 1024
BF16_SUBLANES = 16
MOE_FF_CHUNK = 256
MOE_OUT_CHUNK = 1024


def _mm_kernel(x_ref, w_ref, b_ref, o_ref, *, act):
    acc = jnp.dot(x_ref[...], w_ref[0].astype(BF16), preferred_element_type=F32)
    acc = acc + b_ref[...]
    if act == "gelu":
        acc = jax.nn.gelu(acc)
    o_ref[...] = acc


def _mm(x, w, layer=0, bias=None, act=None):
    M, K = x.shape
    N = w.shape[2]
    n_pad = -N % 128
    if n_pad:
        w = jnp.pad(w[layer:layer + 1], ((0, 0), (0, 0), (0, n_pad)))
        layer = 0
        if bias is not None:
            bias = jnp.pad(bias, (0, n_pad))
    Np = N + n_pad
    if M >= 1024:
        tm = 1024 if K <= 4096 else 512
    else:
        tm = M + (-M % BF16_SUBLANES)
    m_pad = -M % tm
    xb = jnp.pad(x, ((0, m_pad), (0, 0))).astype(BF16)
    Mp = M + m_pad
    if Mp // tm > 1:
        w = w[layer:layer + 1].astype(BF16)
        layer = 0
    tn = 128
    for cand in (512, 256):
        if Np % cand == 0:
            tn = cand
            break
    b2 = jnp.zeros((1, Np), F32) if bias is None else bias.reshape(1, Np).astype(F32)
    out = pl.pallas_call(
        functools.partial(_mm_kernel, act=act),
        grid=(Mp // tm, Np // tn),
        in_specs=[
            pl.BlockSpec((tm, K), lambda i, j: (i, 0)),
            pl.BlockSpec((1, K, tn), lambda i, j: (layer, 0, j)),
            pl.BlockSpec((1, tn), lambda i, j: (0, j)),
        ],
        out_specs=pl.BlockSpec((tm, tn), lambda i, j: (i, j)),
        out_shape=jax.ShapeDtypeStruct((Mp, Np), F32),
        compiler_params=pltpu.CompilerParams(
            dimension_semantics=("parallel", "arbitrary"),
            vmem_limit_bytes=VMEM_LIMIT_BYTES),
        name="dense_mm",
    )(xb, w, b2)
    return out[:M, :N]


def _is_new_expert(blk_e_ref, i):
    prev = blk_e_ref[jnp.maximum(i - 1, 0)]
    return jnp.logical_or(i == 0, blk_e_ref[i] != prev)


def _moe_up_kernel(blk_e_ref, nused_ref, x_ref, w1_ref, w3_ref, h_ref, wb_ref):
    i = pl.program_id(1)
    valid = i < nused_ref[0]

    @pl.when(valid)
    def _():
        @pl.when(_is_new_expert(blk_e_ref, i))
        def _():
            wb_ref[0] = w1_ref[0, 0].astype(BF16)
            wb_ref[1] = w3_ref[0, 0].astype(BF16)

        x = x_ref[...]
        a = jnp.dot(x, wb_ref[0], preferred_element_type=F32)
        g = jnp.dot(x, wb_ref[1], preferred_element_type=F32)
        h_ref[...] = (a * jax.nn.sigmoid(a) * g).astype(h_ref.dtype)

    @pl.when(jnp.logical_not(valid))
    def _():
        h_ref[...] = jnp.zeros_like(h_ref)


def _moe_down_kernel(blk_e_ref, nused_ref, h_ref, g_ref, w2_ref, y_ref, wb_ref):
    i = pl.program_id(1)
    valid = i < nused_ref[0]

    @pl.when(valid)
    def _():
        @pl.when(_is_new_expert(blk_e_ref, i))
        def _():
            wb_ref[...] = w2_ref[0, 0].astype(BF16)

        y = jnp.dot(h_ref[...], wb_ref[...], preferred_element_type=F32)
        y_ref[...] = y * g_ref[...]

    @pl.when(jnp.logical_not(valid))
    def _():
        y_ref[...] = jnp.zeros_like(y_ref)


def _grouped_ffn(xs, row_g, blk_e, nused, w1, w3, w2, layer, rows_per_block):
    P, D = xs.shape
    F = w1.shape[3]
    R = rows_per_block
    nb = P // R
    fc = MOE_FF_CHUNK
    oc = MOE_OUT_CHUNK

    def row_map(j, i, blk_e_ref, nused_ref):
        return (jnp.minimum(i, nused_ref[0] - 1), 0)

    def expert_of(i, blk_e_ref, nused_ref):
        return blk_e_ref[jnp.minimum(i, nused_ref[0] - 1)]

    h = pl.pallas_call(
        _moe_up_kernel,
        grid_spec=pltpu.PrefetchScalarGridSpec(
            num_scalar_prefetch=2,
            grid=(F // fc, nb),
            in_specs=[
                pl.BlockSpec((R, D), row_map),
                pl.BlockSpec((1, 1, D, fc), lambda j, i, be, nu: (layer, expert_of(i, be, nu), 0, j)),
                pl.BlockSpec((1, 1, D, fc), lambda j, i, be, nu: (layer, expert_of(i, be, nu), 0, j)),
            ],
            out_specs=pl.BlockSpec((R, fc), lambda j, i, be, nu: (i, j)),
            scratch_shapes=[pltpu.VMEM((2, D, fc), BF16)],
        ),
        out_shape=jax.ShapeDtypeStruct((P, F), BF16),
        compiler_params=pltpu.CompilerParams(
            dimension_semantics=("arbitrary", "arbitrary"),
            vmem_limit_bytes=VMEM_LIMIT_BYTES),
        name="moe_up",
    )(blk_e, nused, xs, w1, w3)

    y = pl.pallas_call(
        _moe_down_kernel,
        grid_spec=pltpu.PrefetchScalarGridSpec(
            num_scalar_prefetch=2,
            grid=(D // oc, nb),
            in_specs=[
                pl.BlockSpec((R, F), row_map),
                pl.BlockSpec((R, 1), row_map),
                pl.BlockSpec((1, 1, F, oc), lambda n, i, be, nu: (layer, expert_of(i, be, nu), 0, n)),
            ],
            out_specs=pl.BlockSpec((R, oc), lambda n, i, be, nu: (i, n)),
            scratch_shapes=[pltpu.VMEM((F, oc), BF16)],
        ),
        out_shape=jax.ShapeDtypeStruct((P, D), F32),
        compiler_params=pltpu.CompilerParams(
            dimension_semantics=("arbitrary", "arbitrary"),
            vmem_limit_bytes=VMEM_LIMIT_BYTES),
        name="moe_down",
    )(blk_e, nused, h, row_g, w2)
    return y


def _routed_experts(xt, eidx, gw, w1, w3, w2, layer):
    N, D = xt.shape
    P = N * TOP_K
    R = 256 if N >= 256 else BF16_SUBLANES
    rows = -(-(P + N_EXPERTS * (R - 1)) // R) * R
    nblk = rows // R
    flat_e = eidx.reshape(P).astype(jnp.int32)
    onehot = (flat_e[:, None] == jnp.arange(N_EXPERTS, dtype=jnp.int32)[None, :])
    cs = 256 if P % 256 == 0 else P
    oh3 = onehot.reshape(P // cs, cs, N_EXPERTS).astype(BF16)
    tri = (jnp.arange(cs)[:, None] > jnp.arange(cs)[None, :]).astype(BF16)
    inner = jnp.einsum('ts,cse->cte', tri, oh3, preferred_element_type=F32)
    chunk_tot = jnp.sum(oh3.astype(F32), axis=1)
    chunk_off = jnp.cumsum(chunk_tot, axis=0) - chunk_tot
    prefix = (inner + chunk_off[:, None, :]).reshape(P, N_EXPERTS)
    rank = jnp.sum(jnp.where(onehot, prefix, 0.0), axis=1).astype(jnp.int32)
    counts = jnp.sum(chunk_tot, axis=0).astype(jnp.int32)
    padded = (counts + R - 1) // R * R
    pends = jnp.cumsum(padded)
    pstarts = pends - padded
    dest = pstarts[flat_e] + rank
    pair_tok = jnp.arange(P, dtype=jnp.int32) // TOP_K
    row_tok = jnp.full((rows,), N, dtype=jnp.int32).at[dest].set(pair_tok)
    row_g = jnp.zeros((rows,), F32).at[dest].set(gw.reshape(P))
    blk_e = jnp.clip(jnp.searchsorted(pends, jnp.arange(nblk, dtype=jnp.int32) * R, side='right'),
                     0, N_EXPERTS - 1).astype(jnp.int32)
    nused = (pends[-1] // R).astype(jnp.int32).reshape(1)
    x_pad = jnp.concatenate([xt.astype(BF16), jnp.zeros((1, D), BF16)], axis=0)
    xs = x_pad[row_tok]
    yb = _grouped_ffn(xs, row_g.reshape(rows, 1), blk_e, nused, w1, w3, w2, layer, R)
    return jnp.sum(yb[dest].reshape(N, TOP_K, D), axis=1)


def _shared_expert(xt, ws1, ws3, ws2, layer):
    N, D = xt.shape
    m_pad = -N % BF16_SUBLANES
    rows = N + m_pad
    R = 256 if rows % 256 == 0 else rows
    xs = jnp.pad(xt, ((0, m_pad), (0, 0))).astype(BF16)
    nblk = rows // R
    y = _grouped_ffn(xs, jnp.ones((rows, 1), F32), jnp.zeros((nblk,), jnp.int32),
                     jnp.full((1,), nblk, jnp.int32),
                     ws1[:, None], ws3[:, None], ws2[:, None], layer, R)
    return y[:N]


def _moe_ffn(h, p, layer):
    B, T, D = h.shape
    N = B * T
    xt = h.reshape(N, D)
    s = jax.nn.sigmoid(_mm(xt, p['w_router'], layer))
    sel = s + p['b_router'][layer].astype(F32)
    gsc = jnp.sum(lax.top_k(sel.reshape(N, N_GROUPS, N_EXPERTS // N_GROUPS), 2)[0], axis=-1)
    _, gidx = lax.top_k(gsc, TOPK_GROUPS)
    gmask = jnp.sum(jax.nn.one_hot(gidx, N_GROUPS, dtype=F32), axis=1)
    emask = jnp.repeat(gmask, N_EXPERTS // N_GROUPS, axis=1) > 0
    _, eidx = lax.top_k(jnp.where(emask, sel, NEG_INF), TOP_K)
    gw = jnp.take_along_axis(s, eidx, axis=1)
    gw = gw / jnp.sum(gw, axis=-1, keepdims=True) * ROUTE_SCALE
    routed = _routed_experts(xt, eidx, gw, p['w_exp1'], p['w_exp3'], p['w_exp2'], layer)
    shared = _shared_expert(xt, p['w_sh1'], p['w_sh3'], p['w_sh2'], layer)
    return (routed + shared).reshape(B, T, D)


def _rmsnorm(x, g):
    y = x * lax.rsqrt(jnp.mean(x * x, axis=-1, keepdims=True) + RMS_EPS)
    return y * g


def _layernorm(x, g, b):
    xc = x - jnp.mean(x, axis=-1, keepdims=True)
    y = xc * lax.rsqrt(jnp.mean(xc * xc, axis=-1, keepdims=True) + LN_EPS)
    return y * g + b


def _masked_softmax(s, mask):
    maskf = mask.astype(F32)
    s = jnp.where(mask, s, NEG_INF)
    e = jnp.exp(s - jnp.max(s, axis=-1, keepdims=True)) * maskf
    d = jnp.sum(e, axis=-1, keepdims=True)
    return e / jnp.where(d > 0, d, 1.0)


def _alibi_slopes():
    h = jnp.arange(1, N_HEADS + 1, dtype=F32)
    return (2.0 ** (-8.0 * h / N_HEADS)).reshape(N_KV_HEADS, HEADS_PER_KV)


def _gmlp_mixer(h, p, l):
    B, T, D = h.shape
    n = CHUNK if T % CHUNK == 0 else T
    uv = _mm(h.reshape(B * T, D), p['w_gmlp_in'], l, bias=p['b_gmlp_in'][l], act="gelu").reshape(B, T, 2 * GMLP_DFF)
    u, v = jnp.split(uv, 2, axis=-1)
    v = _layernorm(v, p['g_sgu_ln'][l], p['b_sgu_ln'][l])
    vg = v.reshape(B, T // n, n, GMLP_GROUPS, GMLP_GDIM)
    causal = jnp.tril(jnp.ones((n, n), dtype=bool))
    ws = jnp.where(causal, p['w_sgu'][l][:, :n, :n], 0.0)
    bias = p['b_sgu'][l][:, :n].T[None, None, :, :, None]
    mixed = jnp.einsum('gts,bcsgd->bctgd', ws, vg) + bias
    out = _mm((u * mixed.reshape(B, T, GMLP_DFF)).reshape(B * T, GMLP_DFF), p['w_gmlp_out'], l)
    return out.reshape(B, T, D), v


def _compress(k, pe, w1, b1, w2):
    B, L = k.shape[:2]
    nseg = L // CMP_STRIDE
    n_sub = CMP_BLOCK // CMP_STRIDE
    nc = nseg - n_sub + 1
    seg = k[:, :nseg * CMP_STRIDE].reshape(B, nseg, CMP_STRIDE, N_KV_HEADS, HEAD_DIM)
    blocks = jnp.concatenate([seg[:, i:i + nc] for i in range(n_sub)], axis=2)
    blocks = blocks + pe[None, None, :, None, :]
    flat = blocks.transpose(0, 1, 3, 2, 4).reshape(B * nc * N_KV_HEADS, CMP_BLOCK * HEAD_DIM)
    hid = _mm(flat, w1[None], 0, bias=b1, act="gelu")
    return _mm(hid, w2[None], 0).reshape(B, nc, N_KV_HEADS, HEAD_DIM)


def _kv_side(kv_rows, p, past):
    B, T = kv_rows.shape[:2]
    cmp_all, slc_all, win_rows = kv_rows[:, :, 0], kv_rows[:, :, 1], kv_rows[:, :, 2]
    tail = win_rows.shape[2:]
    if past is None:
        win_prev = jnp.zeros((B, WINDOW) + tail, win_rows.dtype)
    else:
        cmp_past, slc_past, win_buf = past
        cmp_all = jnp.concatenate([cmp_past, cmp_all], axis=1)
        slc_all = jnp.concatenate([slc_past, slc_all], axis=1)
        win_prev = jnp.concatenate([jnp.zeros((B, WINDOW - win_buf.shape[1]) + tail, win_rows.dtype), win_buf], axis=1)
    win_all = jnp.concatenate([win_prev, win_rows], axis=1)
    L = cmp_all.shape[1]
    kc = _compress(cmp_all[:, :, 0], p['pe_cmp'][0], p['w_cmp1'][0], p['b_cmp1'][0], p['w_cmp2'][0])
    vc = _compress(cmp_all[:, :, 1], p['pe_cmp'][1], p['w_cmp1'][1], p['b_cmp1'][1], p['w_cmp2'][1])
    nc = kc.shape[1]
    cpos = jnp.arange(nc) * CMP_STRIDE + CMP_BLOCK - 1
    nsb = -(-L // SEL_BLOCK)
    slc_pad = jnp.pad(slc_all, ((0, 0), (0, nsb * SEL_BLOCK - L), (0, 0), (0, 0), (0, 0)))
    blk = slc_pad.reshape(B, nsb, SEL_BLOCK, 2, N_KV_HEADS, HEAD_DIM).transpose(3, 0, 4, 1, 2, 5)
    ci = jnp.arange(nc)[:, None] * CMP_STRIDE
    sj = jnp.arange(nsb)[None, :] * SEL_BLOCK
    overlap = ((ci < sj + SEL_BLOCK) & (ci + CMP_BLOCK > sj)).astype(F32)
    return dict(kc=kc, vc=vc, cpos=cpos, overlap=overlap, ks_blk=blk[0], vs_blk=blk[1],
                kw=win_all[:, :, 0], vw=win_all[:, :, 1])


def _nsa_attend(q, g, pos_q, kc, vc, cpos, overlap, ks_blk, vs_blk, kw, vw, wpos):
    B, Q = q.shape[:2]
    G = N_KV_HEADS
    nsb = ks_blk.shape[2]
    scale = HEAD_DIM ** -0.5
    slopes = _alibi_slopes()[None, None, :, :, None]
    dc = (pos_q[:, None] - cpos[None, :]).astype(F32)
    sc = jnp.einsum('bqgrd,bcgd->bqgrc', q, kc) * scale - slopes * dc[None, :, None, None, :]
    pc = _masked_softmax(sc, (dc >= 0)[None, :, None, None, :])
    o_c = jnp.einsum('bqgrc,bcgd->bqgrd', pc, vc)
    imp = jnp.einsum('bqgrc,cj->bqgj', pc, overlap)
    jt = pos_q // SEL_BLOCK
    jj = jnp.arange(nsb)
    valid = jj[None, :] <= jt[:, None]
    forced = (jj[None, :] == 0) | (jj[None, :] == jt[:, None]) | (jj[None, :] == jt[:, None] - 1)
    imp = jnp.where(forced[None, :, None, :], FORCE, jnp.where(valid[None, :, None, :], imp, -FORCE))
    _, top = lax.top_k(imp, min(N_SEL, nsb))
    top_g = top.transpose(0, 2, 1, 3)
    take = jax.vmap(jax.vmap(lambda blk, ix: blk[ix]))
    nk = top_g.shape[-1] * SEL_BLOCK
    kg = take(ks_blk, top_g).reshape(B, G, Q, nk, HEAD_DIM)
    vg = take(vs_blk, top_g).reshape(B, G, Q, nk, HEAD_DIM)
    kpos = (top_g[..., None] * SEL_BLOCK + jnp.arange(SEL_BLOCK)).reshape(B, G, Q, nk)
    ds = (pos_q[None, None, :, None] - kpos).astype(F32).transpose(0, 2, 1, 3)[:, :, :, None, :]
    ss = jnp.einsum('bqgrd,bgqnd->bqgrn', q, kg) * scale - slopes * ds
    ps = _masked_softmax(ss, ds >= 0)
    o_s = jnp.einsum('bqgrn,bgqnd->bqgrd', ps, vg)
    dw = (pos_q[:, None] - wpos[None, :]).astype(F32)
    mw = (dw >= 0) & (dw <= WINDOW) & (wpos >= 0)[None, :]
    sw = jnp.einsum('bqgrd,blgd->bqgrl', q, kw) * scale - slopes * dw[None, :, None, None, :]
    pw = _masked_softmax(sw, mw[None, :, None, None, :])
    o_w = jnp.einsum('bqgrl,blgd->bqgrd', pw, vw)
    o = g[..., 0:1] * o_c + g[..., 1:2] * o_s + g[..., 2:3] * o_w
    return o.reshape(B, Q, N_HEADS * HEAD_DIM)


NSA_TQ = 256
NSA_TK = 512
NSA_NC_PAD = 128


def _flash_update(qr, k, v, mask, dist, slope, m_ref, l_ref, a_ref, r):
    s = lax.dot_general(qr, k, (((1,), (1,)), ((), ())), preferred_element_type=F32)
    s = s * (HEAD_DIM ** -0.5) - slope * dist
    s = jnp.where(mask, s, NEG_INF)
    m_old = m_ref[r]
    m_new = jnp.maximum(m_old, jnp.max(s, axis=-1, keepdims=True))
    alpha = jnp.exp(m_old - m_new)
    e = jnp.where(mask, jnp.exp(s - m_new), 0.0)
    l_ref[r] = alpha * l_ref[r] + jnp.sum(e, axis=-1, keepdims=True)
    a_ref[r] = alpha * a_ref[r] + jnp.dot(e.astype(BF16), v, preferred_element_type=F32)
    m_ref[r] = m_new


def _nsa_prompt_kernel(slopes_ref, q_ref, g_ref, kc_ref, vc_ref, ks_ref, vs_ref, kw_ref, vw_ref,
                       ov_ref, ex_ref, o_ref,
                       sel_ref, oc_ref, ms_ref, ls_ref, as_ref, mw_ref, lw_ref, aw_ref, *, n_cmp):
    g = pl.program_id(1)
    qi = pl.program_id(2)
    kt = pl.program_id(3)
    tq = q_ref.shape[0]
    tk = ks_ref.shape[0]
    nsb = ov_ref.shape[1]
    q0 = qi * tq
    last = q0 // tk
    pos_q = q0 + lax.broadcasted_iota(jnp.int32, (tq, 1), 0)

    @pl.when(kt == 0)
    def _():
        for ref in (ms_ref, mw_ref):
            ref[...] = jnp.full(ref.shape, NEG_INF, F32)
        for ref in (ls_ref, lw_ref, as_ref, aw_ref):
            ref[...] = jnp.zeros(ref.shape, F32)
        cidx = lax.broadcasted_iota(jnp.int32, (1, NSA_NC_PAD), 1)
        dc = (pos_q - (cidx * CMP_STRIDE + CMP_BLOCK - 1)).astype(F32)
        mc = jnp.logical_and(dc >= 0, cidx < n_cmp)
        kc = kc_ref[0, 0]
        vc = vc_ref[0, 0]
        ov = ov_ref[...]
        imp = jnp.zeros((tq, nsb), F32)
        for r in range(HEADS_PER_KV):
            qr = q_ref[:, r * HEAD_DIM:(r + 1) * HEAD_DIM]
            s = lax.dot_general(qr, kc, (((1,), (1,)), ((), ())), preferred_element_type=F32)
            s = s * (HEAD_DIM ** -0.5) - slopes_ref[g * HEADS_PER_KV + r] * dc
            s = jnp.where(mc, s, NEG_INF)
            e = jnp.where(mc, jnp.exp(s - jnp.max(s, axis=-1, keepdims=True)), 0.0)
            d = jnp.sum(e, axis=-1, keepdims=True)
            pb = (e / jnp.where(d > 0, d, 1.0)).astype(BF16)
            oc_ref[r] = jnp.dot(pb, vc, preferred_element_type=F32)
            imp = imp + jnp.dot(pb, ov, preferred_element_type=F32)
        jj = lax.broadcasted_iota(jnp.int32, (1, nsb), 1)
        jt = pos_q // SEL_BLOCK
        forced = jnp.logical_or(jj == 0, jnp.logical_or(jj == jt, jj == jt - 1))
        imp = jnp.where(forced, FORCE, jnp.where(jj <= jt, imp, -FORCE))
        rank = jnp.zeros((tq, nsb), F32)
        for j2 in range(nsb):
            col = imp[:, j2:j2 + 1]
            tie = jnp.where(jj > j2, 1.0, 0.0)
            rank = rank + jnp.where(col > imp, 1.0, jnp.where(col == imp, tie, 0.0))
        sel_ref[...] = jnp.where(rank < N_SEL, 1.0, 0.0).astype(BF16)

    @pl.when(kt <= last)
    def _():
        kpos = kt * tk + lax.broadcasted_iota(jnp.int32, (1, tk), 1)
        dist = (pos_q - kpos).astype(F32)
        selk = jnp.dot(sel_ref[...], ex_ref[...], preferred_element_type=F32)
        msel = jnp.logical_and(selk > 0.5, dist >= 0)
        ks = ks_ref[...].astype(BF16)
        vs = vs_ref[...].astype(BF16)
        for r in range(HEADS_PER_KV):
            qr = q_ref[:, r * HEAD_DIM:(r + 1) * HEAD_DIM]
            _flash_update(qr, ks, vs, msel, dist, slopes_ref[g * HEADS_PER_KV + r], ms_ref, ls_ref, as_ref, r)

        @pl.when(kt >= last - 1)
        def _():
            mwin = jnp.logical_and(dist >= 0, dist <= WINDOW)
            kw = kw_ref[...].astype(BF16)
            vw = vw_ref[...].astype(BF16)
            for r in range(HEADS_PER_KV):
                qr = q_ref[:, r * HEAD_DIM:(r + 1) * HEAD_DIM]
                _flash_update(qr, kw, vw, mwin, dist, slopes_ref[g * HEADS_PER_KV + r], mw_ref, lw_ref, aw_ref, r)

    @pl.when(kt == last)
    def _():
        gate = jax.nn.sigmoid(g_ref[...])
        for r in range(HEADS_PER_KV):
            ls = ls_ref[r]
            lw = lw_ref[r]
            o = (gate[:, 3 * r:3 * r + 1] * oc_ref[r]
                 + gate[:, 3 * r + 1:3 * r + 2] * (as_ref[r] / jnp.where(ls > 0, ls, 1.0))
                 + gate[:, 3 * r + 2:3 * r + 3] * (aw_ref[r] / jnp.where(lw > 0, lw, 1.0)))
            o_ref[:, r * HEAD_DIM:(r + 1) * HEAD_DIM] = o.astype(o_ref.dtype)


def _nsa_prompt(q, glog, kv2d, kc, vc, B, T):
    G, R, DH = N_KV_HEADS, HEADS_PER_KV, HEAD_DIM
    tq, tk = NSA_TQ, NSA_TK
    nq, nkt = T // tq, T // tk
    nsb = T // SEL_BLOCK
    n_cmp = T // CMP_STRIDE - CMP_BLOCK // CMP_STRIDE + 1
    hh = jnp.arange(1, N_HEADS + 1, dtype=F32)
    slopes = 2.0 ** (-8.0 * hh / N_HEADS)
    ci = jnp.arange(NSA_NC_PAD)[:, None] * CMP_STRIDE
    sj = jnp.arange(nsb)[None, :] * SEL_BLOCK
    overlap = ((ci < sj + SEL_BLOCK) & (ci + CMP_BLOCK > sj) & (jnp.arange(NSA_NC_PAD)[:, None] < n_cmp)).astype(BF16)
    expand = (jnp.arange(nsb)[:, None] == jnp.arange(T)[None, :] // SEL_BLOCK).astype(BF16)

    def last_tile(qi):
        return (qi * tq) // tk

    def sel_rows(b, g, qi, kt):
        return b * nkt + jnp.minimum(kt, last_tile(qi))

    def win_rows(b, g, qi, kt):
        last = last_tile(qi)
        return b * nkt + jnp.clip(kt, jnp.maximum(last - 1, 0), last)

    def kv_spec(rows, branch, kv):
        col0 = (branch * 2 + kv) * G
        return pl.BlockSpec((tk, DH), lambda b, g, qi, kt: (rows(b, g, qi, kt), col0 + g))

    return pl.pallas_call(
        functools.partial(_nsa_prompt_kernel, n_cmp=n_cmp),
        grid=(B, G, nq, nkt),
        in_specs=[
            pl.BlockSpec(memory_space=pltpu.SMEM),
            pl.BlockSpec((tq, R * DH), lambda b, g, qi, kt: (b * nq + qi, g)),
            pl.BlockSpec((None, tq, R * N_BRANCH), lambda b, g, qi, kt: (g, b * nq + qi, 0)),
            pl.BlockSpec((1, 1, NSA_NC_PAD, DH), lambda b, g, qi, kt: (b, g, 0, 0)),
            pl.BlockSpec((1, 1, NSA_NC_PAD, DH), lambda b, g, qi, kt: (b, g, 0, 0)),
            kv_spec(sel_rows, 1, 0), kv_spec(sel_rows, 1, 1),
            kv_spec(win_rows, 2, 0), kv_spec(win_rows, 2, 1),
            pl.BlockSpec((NSA_NC_PAD, nsb), lambda b, g, qi, kt: (0, 0)),
            pl.BlockSpec((nsb, tk), lambda b, g, qi, kt: (0, jnp.minimum(kt, last_tile(qi)))),
        ],
        out_specs=pl.BlockSpec((tq, R * DH), lambda b, g, qi, kt: (b * nq + qi, g)),
        out_shape=jax.ShapeDtypeStruct((B * T, N_HEADS * DH), BF16),
        scratch_shapes=[
            pltpu.VMEM((tq, nsb), BF16),
            pltpu.VMEM((R, tq, DH), F32),
            pltpu.VMEM((R, tq, 1), F32), pltpu.VMEM((R, tq, 1), F32), pltpu.VMEM((R, tq, DH), F32),
            pltpu.VMEM((R, tq, 1), F32), pltpu.VMEM((R, tq, 1), F32), pltpu.VMEM((R, tq, DH), F32),
        ],
        compiler_params=pltpu.CompilerParams(
            dimension_semantics=("parallel", "parallel", "parallel", "arbitrary"),
            vmem_limit_bytes=VMEM_LIMIT_BYTES),
        name="nsa_prompt",
    )(slopes, q, glog, kc, vc, kv2d, kv2d, kv2d, kv2d, overlap, expand)


def _nsa_mixer_prompt(hn, kv2d, kc, vc, p, j):
    B, T, D = hn.shape
    HD = N_HEADS * HEAD_DIM
    G, R = N_KV_HEADS, HEADS_PER_KV
    qg = _mm(hn.reshape(B * T, D), p['w_qg'], j)
    q = qg[:, :HD].astype(BF16)
    glog = qg[:, HD:].reshape(B * T, G, R * N_BRANCH).transpose(1, 0, 2)
    pad = NSA_NC_PAD - kc.shape[1]
    kcp = jnp.pad(kc, ((0, 0), (0, pad), (0, 0), (0, 0))).transpose(0, 2, 1, 3).astype(BF16)
    vcp = jnp.pad(vc, ((0, 0), (0, pad), (0, 0), (0, 0))).transpose(0, 2, 1, 3).astype(BF16)
    o = _nsa_prompt(q, glog, kv2d, kcp, vcp, B, T)
    return _mm(o, p['w_o'], j).reshape(B, T, D)


def _nsa_mixer(hn, ctx, p, j, pos0):
    B, T, D = hn.shape
    HD = N_HEADS * HEAD_DIM
    qg = _mm(hn.reshape(B * T, D), p['w_qg'], j).reshape(B, T, HD + N_BRANCH * N_HEADS)
    q = qg[..., :HD].reshape(B, T, N_KV_HEADS, HEADS_PER_KV, HEAD_DIM)
    gates = jax.nn.sigmoid(qg[..., HD:]).reshape(B, T, N_KV_HEADS, HEADS_PER_KV, N_BRANCH)
    QB = NSA_QBLK if T % NSA_QBLK == 0 else T

    def attend_block(b0):
        pos_q = pos0 + b0 + jnp.arange(QB)
        wpos = pos0 + b0 - WINDOW + jnp.arange(WINDOW + QB)
        return _nsa_attend(lax.dynamic_slice_in_dim(q, b0, QB, axis=1),
                           lax.dynamic_slice_in_dim(gates, b0, QB, axis=1),
                           pos_q, ctx['kc'], ctx['vc'], ctx['cpos'], ctx['overlap'],
                           ctx['ks_blk'], ctx['vs_blk'],
                           lax.dynamic_slice_in_dim(ctx['kw'], b0, WINDOW + QB, axis=1),
                           lax.dynamic_slice_in_dim(ctx['vw'], b0, WINDOW + QB, axis=1), wpos)

    o = lax.map(attend_block, jnp.arange(T // QB) * QB)
    o = jnp.moveaxis(o, 0, 1).reshape(B * T, HD)
    return _mm(o, p['w_o'], j).reshape(B, T, D)


def _trunk(x, c, p, pos0, past):
    B, T, D = x.shape
    h = x
    c_act = jax.nn.silu(c)
    v_rows, kv_rows, ctx = [], None, None
    for l in range(DEPTH):
        mod = _mm(c_act, p['w_ada'], l, bias=p['b_ada'][l])[:, None, :]
        sh1, sc1, gt1, sh2, sc2, gt2 = jnp.split(mod, 6, axis=-1)
        hn = _rmsnorm(h, p['g_norm_mix'][l]) * (1 + sc1) + sh1
        if l < N_A_LAYERS:
            mix, v = _gmlp_mixer(hn, p, l)
            v_rows.append(v[:, T - min(T, CHUNK):])
        else:
            if ctx is None:
                mkv = _mm(c_act, p['w_ada_kv'][None], 0, bias=p['b_ada_kv'])[:, None, :]
                sh_kv, sc_kv = jnp.split(mkv, 2, axis=-1)
                kvn = _rmsnorm(h, p['g_norm_kv']) * (1 + sc_kv) + sh_kv
                kv2d = _mm(kvn.reshape(B * T, D), p['w_kv'][None], 0)
                kv_rows = kv2d.reshape(B, T, N_BRANCH, 2, N_KV_HEADS, HEAD_DIM)
                if past is None:
                    ctx = tuple(_compress(kv_rows[:, :, 0, i], p['pe_cmp'][i], p['w_cmp1'][i], p['b_cmp1'][i],
                                          p['w_cmp2'][i]) for i in range(2))
                else:
                    ctx = _kv_side(kv_rows, p, past)
            if past is None:
                mix = _nsa_mixer_prompt(hn, kv2d, ctx[0], ctx[1], p, l - N_A_LAYERS)
            else:
                mix = _nsa_mixer(hn, ctx, p, l - N_A_LAYERS, pos0)
        h = h + gt1 * mix
        hn = _rmsnorm(h, p['g_norm_ffn'][l]) * (1 + sc2) + sh2
        h = h + gt2 * _moe_ffn(hn, p, l)
    return _rmsnorm(h, p['g_final']), jnp.stack(v_rows), kv_rows


def kernel(x_prompt, x_sample, c_prompt, c_sample, cache_cmp_kv, cache_slc_kv, state_win_kv, page_table,
           w_ada, b_ada, g_norm_mix, g_norm_ffn, w_gmlp_in, b_gmlp_in, g_sgu_ln, b_sgu_ln, w_sgu, b_sgu,
           w_gmlp_out, w_ada_kv, b_ada_kv, g_norm_kv, w_kv, pe_cmp, w_cmp1, b_cmp1, w_cmp2, w_qg, w_o,
           w_router, b_router, w_exp1, w_exp3, w_exp2, w_sh1, w_sh3, w_sh2, g_final):
    p = dict(w_ada=w_ada, b_ada=b_ada, g_norm_mix=g_norm_mix, g_norm_ffn=g_norm_ffn,
             w_gmlp_in=w_gmlp_in, b_gmlp_in=b_gmlp_in, g_sgu_ln=g_sgu_ln, b_sgu_ln=b_sgu_ln,
             w_sgu=w_sgu, b_sgu=b_sgu, w_gmlp_out=w_gmlp_out, w_ada_kv=w_ada_kv, b_ada_kv=b_ada_kv,
             g_norm_kv=g_norm_kv, w_kv=w_kv, pe_cmp=pe_cmp, w_cmp1=w_cmp1, b_cmp1=b_cmp1, w_cmp2=w_cmp2,
             w_qg=w_qg, w_o=w_o, w_router=w_router, b_router=b_router, w_exp1=w_exp1, w_exp3=w_exp3,
             w_exp2=w_exp2, w_sh1=w_sh1, w_sh3=w_sh3, w_sh2=w_sh2, g_final=g_final)
    y_prompt, v_prompt, kv_p = _trunk(x_prompt, c_prompt, p, 0, None)
    n_seq, n_pages = page_table.shape
    page = cache_cmp_kv.shape[1]
    past_len = n_pages * page
    cmp_past = cache_cmp_kv[page_table].reshape(n_seq, past_len, 2, N_KV_HEADS, HEAD_DIM)
    slc_past = cache_slc_kv[page_table].reshape(n_seq, past_len, 2, N_KV_HEADS, HEAD_DIM)
    y_sample, v_sample, kv_s = _trunk(x_sample, c_sample, p, past_len, (cmp_past, slc_past, state_win_kv))
    B, T = x_prompt.shape[:2]
    cmp_kv_prompt = kv_p[:, :, 0].reshape(B, T // page, page, 2, N_KV_HEADS, HEAD_DIM)
    slc_kv_prompt = kv_p[:, :, 1].reshape(B, T // page, page, 2, N_KV_HEADS, HEAD_DIM)
    win_kv_prompt = kv_p[:, T - min(WINDOW, T):, 2]
    return (y_prompt, y_sample, v_prompt, v_sample, cmp_kv_prompt, kv_s[:, :, 0], slc_kv_prompt,
            kv_s[:, :, 1], win_kv_prompt, kv_s[:, :, 2])
```

```python
import functools

import jax
import jax.numpy as jnp
from jax import lax
from jax.experimental import pallas as pl
from jax.experimental.pallas import tpu as pltpu

D_MODEL = 4096
DEPTH = 2
PAGE_SIZE = 128
N_A_LAYERS = DEPTH // 2
CHUNK = 128
GMLP_DFF = 2 * D_MODEL
GMLP_GROUPS = 16
GMLP_GDIM = GMLP_DFF // GMLP_GROUPS
N_HEADS = 32
HEAD_DIM = D_MODEL // N_HEADS
N_KV_HEADS = 4
HEADS_PER_KV = N_HEADS // N_KV_HEADS
N_BRANCH = 3
CMP_BLOCK = 32
CMP_STRIDE = 16
SEL_BLOCK = 64
N_SEL = 16
WINDOW = 512
N_EXPERTS = 64
TOP_K = 8
N_GROUPS = 8
TOPK_GROUPS = 4
EXPERT_DFF = D_MODEL // 4
ROUTE_SCALE = 2.5
RMS_EPS = 1e-6
LN_EPS = 1e-5
NEG_INF = -1e30
FORCE = 1e30

BF16 = jnp.bfloat16
F32 = jnp.float32

VMEM_LIMIT_BYTES = 56 * 1024 * 1024
BF16_SUBLANES = 16
LANES = 128
MOE_FF_CHUNK = 256
MOE_OUT_CHUNK = 1024
MOE_ROWS = 512
ROW_TILE = 128


def _cdiv(a, b):
    return -(-a // b)


def _round_up(a, b):
    return _cdiv(a, b) * b


def _mm_kernel(x_ref, w_ref, b_ref, o_ref, *, act):
    acc = jnp.dot(x_ref[...], w_ref[0].astype(BF16), preferred_element_type=F32)
    acc = acc + b_ref[...]
    if act == "gelu":
        acc = jax.nn.gelu(acc)
    o_ref[...] = acc.astype(o_ref.dtype)


def _mm(x, w, layer=0, bias=None, act=None, out_dtype=F32):
    M, K = x.shape
    N = w.shape[2]
    assert N % LANES == 0
    xb = x.astype(BF16)
    if M >= 1024:
        n_tiles = _cdiv(M, 1024 if K <= 4096 else 512)
        tm = _round_up(_cdiv(M, n_tiles), BF16_SUBLANES)
        rows = M
    else:
        tm = _round_up(M, BF16_SUBLANES)
        rows = tm
        xb = jnp.pad(xb, ((0, rows - M), (0, 0)))
    if _cdiv(rows, tm) > 1:
        w = w[layer:layer + 1].astype(BF16)
        layer = 0
    tn = 512 if N % 512 == 0 else (256 if N % 256 == 0 else 128)
    b2 = jnp.zeros((1, N), F32) if bias is None else bias.reshape(1, N).astype(F32)
    out = pl.pallas_call(
        functools.partial(_mm_kernel, act=act),
        grid=(_cdiv(rows, tm), N // tn),
        in_specs=[
            pl.BlockSpec((tm, K), lambda i, j: (i, 0)),
            pl.BlockSpec((1, K, tn), lambda i, j: (layer, 0, j)),
            pl.BlockSpec((1, tn), lambda i, j: (0, j)),
        ],
        out_specs=pl.BlockSpec((tm, tn), lambda i, j: (i, j)),
        out_shape=jax.ShapeDtypeStruct((rows, N), out_dtype),
        compiler_params=pltpu.CompilerParams(
            dimension_semantics=("parallel", "arbitrary"),
            vmem_limit_bytes=VMEM_LIMIT_BYTES),
        name="dense_mm",
    )(xb, w, b2)
    return out[:M]


def _resnorm_kernel(*refs, n_res, n_prompt_tiles, tiles_per_seq, n_prompt_seq, write_h, has_router):
    h_ref = refs[0]
    res_refs = refs[1:1 + n_res]
    pos = 1 + n_res
    if n_res:
        gt_ref = refs[pos]
        pos += 1
    g_ref, sc_ref, sh_ref = refs[pos:pos + 3]
    pos += 3
    if has_router:
        wr_ref = refs[pos]
        pos += 1
    outs = list(refs[pos:])
    hnew_ref = outs.pop(0) if (n_res and write_h) else None
    hn_ref = outs.pop(0)
    i = pl.program_id(0)
    tm = h_ref.shape[0]

    def body(rows_of):
        h = h_ref[...]
        if n_res:
            r = res_refs[0][...]
            for extra in res_refs[1:]:
                r = r + extra[...]
            h = h + rows_of(gt_ref) * r
            if hnew_ref is not None:
                hnew_ref[...] = h
        y = h * lax.rsqrt(jnp.mean(h * h, axis=-1, keepdims=True) + RMS_EPS) * g_ref[...]
        hn = y * (1.0 + rows_of(sc_ref)) + rows_of(sh_ref)
        hn_ref[...] = hn.astype(hn_ref.dtype)
        if has_router:
            outs[0][...] = jnp.dot(hn, wr_ref[...], preferred_element_type=F32, precision=lax.Precision.HIGHEST)

    @pl.when(i < n_prompt_tiles)
    def _():
        seq = i // tiles_per_seq
        body(lambda tab: tab[pl.ds(seq, 1), :])

    @pl.when(i >= n_prompt_tiles)
    def _():
        body(lambda tab: tab[pl.ds(n_prompt_seq, tm), :])


def _row_table(m, n_prompt_seq):
    n_sample = m.shape[0] - n_prompt_seq
    return jnp.pad(m, ((0, ROW_TILE - n_sample), (0, 0)))


def _resnorm(h, res, gt, g, sc, sh, n_prompt_seq, seq_len, out_dtype=BF16, write_h=True, w_router=None):
    N, D = h.shape
    tm = ROW_TILE
    n_res = len(res)
    has_router = w_router is not None
    tabs = [_row_table(m, n_prompt_seq) for m in (([gt] if n_res else []) + [sc, sh])]
    tab_rows = tabs[0].shape[0]
    row_spec = pl.BlockSpec((tm, D), lambda i: (i, 0))
    tab_spec = pl.BlockSpec((tab_rows, D), lambda i: (0, 0))
    in_specs = [row_spec] * (1 + n_res) + ([tab_spec] if n_res else []) + [pl.BlockSpec((1, D), lambda i: (0, 0))] \
        + [tab_spec, tab_spec]
    operands = [h, *res, *tabs[:1 if n_res else 0], g.reshape(1, D), *tabs[-2:]]
    out_shape = [jax.ShapeDtypeStruct((N, D), out_dtype)]
    out_specs = [row_spec]
    if n_res and write_h:
        out_shape = [jax.ShapeDtypeStruct((N, D), F32)] + out_shape
        out_specs = [row_spec] + out_specs
    if has_router:
        E = w_router.shape[1]
        in_specs.append(pl.BlockSpec((D, E), lambda i: (0, 0)))
        operands.append(w_router)
        out_shape.append(jax.ShapeDtypeStruct((N, E), F32))
        out_specs.append(pl.BlockSpec((tm, E), lambda i: (i, 0)))
    outs = pl.pallas_call(
        functools.partial(_resnorm_kernel, n_res=n_res, n_prompt_tiles=n_prompt_seq * seq_len // tm,
                          tiles_per_seq=seq_len // tm, n_prompt_seq=n_prompt_seq, write_h=write_h,
                          has_router=has_router),
        grid=(_cdiv(N, tm),),
        in_specs=in_specs,
        out_specs=out_specs,
        out_shape=out_shape,
        compiler_params=pltpu.CompilerParams(
            dimension_semantics=("parallel",), vmem_limit_bytes=VMEM_LIMIT_BYTES),
        name="resnorm",
    )(*operands)
    return outs if len(outs) > 1 else outs[0]


def _rank_desc(v, lane, same_group=None):
    n = v.shape[1]
    rank = jnp.zeros(v.shape, F32)
    for j in range(n):
        col = v[:, j:j + 1]
        tie = jnp.where(lane > j, 1.0, 0.0)
        beats = jnp.where(col > v, 1.0, jnp.where(col == v, tie, 0.0))
        if same_group is not None:
            beats = jnp.where(same_group(j), beats, 0.0)
        rank = rank + beats
    return rank


def _router_kernel(logit_ref, b_ref, tri_ref, eidx_ref, gw_ref):
    gs = N_EXPERTS // N_GROUPS
    s = jax.nn.sigmoid(logit_ref[...])
    sel = s + b_ref[...]
    lane = lax.broadcasted_iota(jnp.int32, (1, N_EXPERTS), 1)
    grp = lane // gs
    rank_g = _rank_desc(sel, lane, same_group=lambda j: grp == j // gs)
    top2 = jnp.where(rank_g < 2, sel, 0.0)
    gsum = [jnp.sum(jnp.where(grp == g, top2, 0.0), axis=-1, keepdims=True) for g in range(N_GROUPS)]
    keep = jnp.zeros(sel.shape, F32)
    for g in range(N_GROUPS):
        before = jnp.zeros(gsum[g].shape, F32)
        for g2 in range(N_GROUPS):
            if g2 == g:
                continue
            wins = jnp.where(gsum[g2] > gsum[g], 1.0, jnp.where(gsum[g2] == gsum[g], 1.0 if g2 < g else 0.0, 0.0))
            before = before + wins
        keep = keep + jnp.where(grp == g, jnp.where(before < TOPK_GROUPS, 1.0, 0.0), 0.0)
    masked = jnp.where(keep > 0.5, sel, NEG_INF)
    chosen = _rank_desc(masked, lane) < TOP_K
    sc = jnp.where(chosen, s, 0.0)
    gd = sc / jnp.sum(sc, axis=-1, keepdims=True) * ROUTE_SCALE
    slot = jnp.dot(jnp.where(chosen, 1.0, 0.0).astype(BF16), tri_ref[...], preferred_element_type=F32)
    lane_f = lane.astype(F32)
    kcol = lax.broadcasted_iota(jnp.int32, (1, TOP_K), 1)
    eidx = jnp.zeros((sel.shape[0], TOP_K), F32)
    gw = jnp.zeros((sel.shape[0], TOP_K), F32)
    for k in range(TOP_K):
        hit = jnp.where(chosen, jnp.where(slot == k, 1.0, 0.0), 0.0)
        eidx = jnp.where(kcol == k, jnp.sum(hit * lane_f, axis=-1, keepdims=True), eidx)
        gw = jnp.where(kcol == k, jnp.sum(hit * gd, axis=-1, keepdims=True), gw)
    eidx_ref[...] = eidx.astype(jnp.int32)
    gw_ref[...] = gw


def _route(logits, b_router, layer):
    N = logits.shape[0]
    tm = ROW_TILE
    tri = (jnp.arange(N_EXPERTS)[:, None] < jnp.arange(N_EXPERTS)[None, :]).astype(BF16)
    return pl.pallas_call(
        _router_kernel,
        grid=(_cdiv(N, tm),),
        in_specs=[
            pl.BlockSpec((tm, N_EXPERTS), lambda i: (i, 0)),
            pl.BlockSpec((1, N_EXPERTS), lambda i: (0, 0)),
            pl.BlockSpec((N_EXPERTS, N_EXPERTS), lambda i: (0, 0)),
        ],
        out_specs=[pl.BlockSpec((tm, TOP_K), lambda i: (i, 0)), pl.BlockSpec((tm, TOP_K), lambda i: (i, 0))],
        out_shape=[jax.ShapeDtypeStruct((N, TOP_K), jnp.int32), jax.ShapeDtypeStruct((N, TOP_K), F32)],
        compiler_params=pltpu.CompilerParams(
            dimension_semantics=("parallel",), vmem_limit_bytes=VMEM_LIMIT_BYTES),
        name="router",
    )(logits, b_router[layer].reshape(1, N_EXPERTS).astype(F32), tri)


def _is_new_expert(blk_e_ref, i):
    prev = blk_e_ref[jnp.maximum(i - 1, 0)]
    return jnp.logical_or(i == 0, blk_e_ref[i] != prev)


def _moe_up_kernel(blk_e_ref, nused_ref, x_ref, w1_ref, w3_ref, h_ref, wb_ref):
    i = pl.program_id(1)
    valid = i < nused_ref[0]

    @pl.when(valid)
    def _():
        @pl.when(_is_new_expert(blk_e_ref, i))
        def _():
            wb_ref[0] = w1_ref[0, 0].astype(BF16)
            wb_ref[1] = w3_ref[0, 0].astype(BF16)

        x = x_ref[...]
        a = jnp.dot(x, wb_ref[0], preferred_element_type=F32)
        g = jnp.dot(x, wb_ref[1], preferred_element_type=F32)
        h_ref[...] = (a * jax.nn.sigmoid(a) * g).astype(h_ref.dtype)

    @pl.when(jnp.logical_not(valid))
    def _():
        h_ref[...] = jnp.zeros_like(h_ref)


def _moe_down_kernel(blk_e_ref, nused_ref, h_ref, g_ref, w2_ref, y_ref, wb_ref):
    i = pl.program_id(1)
    valid = i < nused_ref[0]

    @pl.when(valid)
    def _():
        @pl.when(_is_new_expert(blk_e_ref, i))
        def _():
            wb_ref[...] = w2_ref[0, 0].astype(BF16)

        y = jnp.dot(h_ref[...], wb_ref[...], preferred_element_type=F32)
        y_ref[...] = y * g_ref[...]

    @pl.when(jnp.logical_not(valid))
    def _():
        y_ref[...] = jnp.zeros_like(y_ref)


def _grouped_ffn(xs, row_g, blk_e, nused, w1, w3, w2, layer):
    P, D = xs.shape
    F = w1.shape[3]
    R = MOE_ROWS
    nb = _cdiv(P, R)
    fc = MOE_FF_CHUNK
    oc = MOE_OUT_CHUNK

    def row_map(j, i, blk_e_ref, nused_ref):
        return (jnp.minimum(i, nused_ref[0] - 1), 0)

    def expert_of(i, blk_e_ref, nused_ref):
        return blk_e_ref[jnp.minimum(i, nused_ref[0] - 1)]

    h = pl.pallas_call(
        _moe_up_kernel,
        grid_spec=pltpu.PrefetchScalarGridSpec(
            num_scalar_prefetch=2,
            grid=(F // fc, nb),
            in_specs=[
                pl.BlockSpec((R, D), row_map),
                pl.BlockSpec((1, 1, D, fc), lambda j, i, be, nu: (layer, expert_of(i, be, nu), 0, j)),
                pl.BlockSpec((1, 1, D, fc), lambda j, i, be, nu: (layer, expert_of(i, be, nu), 0, j)),
            ],
            out_specs=pl.BlockSpec((R, fc), lambda j, i, be, nu: (i, j)),
            scratch_shapes=[pltpu.VMEM((2, D, fc), BF16)],
        ),
        out_shape=jax.ShapeDtypeStruct((P, F), BF16),
        compiler_params=pltpu.CompilerParams(
            dimension_semantics=("arbitrary", "arbitrary"),
            vmem_limit_bytes=VMEM_LIMIT_BYTES),
        name="moe_up",
    )(blk_e, nused, xs, w1, w3)

    y = pl.pallas_call(
        _moe_down_kernel,
        grid_spec=pltpu.PrefetchScalarGridSpec(
            num_scalar_prefetch=2,
            grid=(D // oc, nb),
            in_specs=[
                pl.BlockSpec((R, F), row_map),
                pl.BlockSpec((R, 1), row_map),
                pl.BlockSpec((1, 1, F, oc), lambda n, i, be, nu: (layer, expert_of(i, be, nu), 0, n)),
            ],
            out_specs=pl.BlockSpec((R, oc), lambda n, i, be, nu: (i, n)),
            scratch_shapes=[pltpu.VMEM((F, oc), BF16)],
        ),
        out_shape=jax.ShapeDtypeStruct((P, D), F32),
        compiler_params=pltpu.CompilerParams(
            dimension_semantics=("arbitrary", "arbitrary"),
            vmem_limit_bytes=VMEM_LIMIT_BYTES),
        name="moe_down",
    )(blk_e, nused, h, row_g, w2)
    return y


def _routed_experts(xt, eidx, gw, w1, w3, w2, layer):
    N, D = xt.shape
    P = N * TOP_K
    R = MOE_ROWS
    rows = _round_up(P + N_EXPERTS * (R - 1), R)
    nblk = rows // R
    cs = 256
    Pp = _round_up(P, cs)
    flat_e = jnp.pad(eidx.reshape(P).astype(jnp.int32), (0, Pp - P), constant_values=N_EXPERTS)
    onehot = (flat_e[:, None] == jnp.arange(N_EXPERTS, dtype=jnp.int32)[None, :])
    oh3 = onehot.reshape(Pp // cs, cs, N_EXPERTS).astype(BF16)
    tri = (jnp.arange(cs)[:, None] > jnp.arange(cs)[None, :]).astype(BF16)
    inner = jnp.einsum('ts,cse->cte', tri, oh3, preferred_element_type=F32)
    chunk_tot = jnp.sum(oh3.astype(F32), axis=1)
    chunk_off = jnp.cumsum(chunk_tot, axis=0) - chunk_tot
    prefix = (inner + chunk_off[:, None, :]).reshape(Pp, N_EXPERTS)
    rank = jnp.sum(jnp.where(onehot, prefix, 0.0), axis=1).astype(jnp.int32)[:P]
    counts = jnp.sum(chunk_tot, axis=0).astype(jnp.int32)
    padded = (counts + R - 1) // R * R
    pends = jnp.cumsum(padded)
    pstarts = pends - padded
    dest = pstarts[flat_e[:P]] + rank
    pair_tok = jnp.arange(P, dtype=jnp.int32) // TOP_K
    row_tok = jnp.full((rows,), N, dtype=jnp.int32).at[dest].set(pair_tok)
    row_g = jnp.zeros((rows,), F32).at[dest].set(gw.reshape(P))
    blk_e = jnp.clip(jnp.searchsorted(pends, jnp.arange(nblk, dtype=jnp.int32) * R, side='right'),
                     0, N_EXPERTS - 1).astype(jnp.int32)
    nused = (pends[-1] // R).astype(jnp.int32).reshape(1)
    x_pad = jnp.concatenate([xt, jnp.zeros((1, D), BF16)], axis=0)
    xs = x_pad[row_tok]
    yb = _grouped_ffn(xs, row_g.reshape(rows, 1), blk_e, nused, w1, w3, w2, layer)
    return jnp.sum(yb[dest].reshape(N, TOP_K, D), axis=1)


def _shared_expert(xt, ws1, ws3, ws2, layer):
    N, D = xt.shape
    nblk = _cdiv(N, MOE_ROWS)
    return _grouped_ffn(xt, jnp.ones((N, 1), F32), jnp.zeros((nblk,), jnp.int32), jnp.full((1,), nblk, jnp.int32),
                        ws1[:, None], ws3[:, None], ws2[:, None], layer)


def _moe_ffn(hn, logits, p, layer):
    eidx, gw = _route(logits, p['b_router'], layer)
    routed = _routed_experts(hn, eidx, gw, p['w_exp1'], p['w_exp3'], p['w_exp2'], layer)
    shared = _shared_expert(hn, p['w_sh1'], p['w_sh3'], p['w_sh2'], layer)
    return routed, shared


def _layernorm(x, g, b):
    xc = x - jnp.mean(x, axis=-1, keepdims=True)
    y = xc * lax.rsqrt(jnp.mean(xc * xc, axis=-1, keepdims=True) + LN_EPS)
    return y * g + b


def _gmlp_mixer(hn, p, l, n_prompt_seq, seq_len):
    N, D = hn.shape
    Np = n_prompt_seq * seq_len
    uv = _mm(hn, p['w_gmlp_in'], l, bias=p['b_gmlp_in'][l], act="gelu")
    u, v = uv[:, :GMLP_DFF], uv[:, GMLP_DFF:]
    v = _layernorm(v, p['g_sgu_ln'][l], p['b_sgu_ln'][l])
    n = CHUNK
    vg = v[:Np].reshape(n_prompt_seq, seq_len // n, n, GMLP_GROUPS, GMLP_GDIM)
    causal = jnp.tril(jnp.ones((n, n), dtype=bool))
    ws = jnp.where(causal, p['w_sgu'][l], 0.0)
    bias = p['b_sgu'][l].T[None, None, :, :, None]
    mixed_p = (jnp.einsum('gts,bcsgd->bctgd', ws, vg) + bias).reshape(Np, GMLP_DFF)
    vs = v[Np:].reshape(N - Np, GMLP_GROUPS, GMLP_GDIM)
    w00 = p['w_sgu'][l][:, 0, 0].astype(BF16).astype(F32)[None, :, None]
    mixed_s = (w00 * vs.astype(BF16).astype(F32) + p['b_sgu'][l][:, 0][None, :, None]).reshape(N - Np, GMLP_DFF)
    um = u * jnp.concatenate([mixed_p, mixed_s], axis=0)
    return _mm(um, p['w_gmlp_out'], l), v


def _compress(k, pe, w1, b1, w2):
    B, L = k.shape[:2]
    nseg = L // CMP_STRIDE
    n_sub = CMP_BLOCK // CMP_STRIDE
    nc = nseg - n_sub + 1
    seg = k[:, :nseg * CMP_STRIDE].reshape(B, nseg, CMP_STRIDE, N_KV_HEADS, HEAD_DIM)
    blocks = jnp.concatenate([seg[:, i:i + nc] for i in range(n_sub)], axis=2)
    blocks = blocks + pe[None, None, :, None, :]
    flat = blocks.transpose(0, 1, 3, 2, 4).reshape(B * nc * N_KV_HEADS, CMP_BLOCK * HEAD_DIM)
    hid = _mm(flat, w1[None], 0, bias=b1, act="gelu")
    return _mm(hid, w2[None], 0).reshape(B, nc, N_KV_HEADS, HEAD_DIM)


NSA_TQ = 256
NSA_TK = 512
NSA_NC_PAD = 128


def _flash_update(qr, k, v, mask, dist, slope, m_ref, l_ref, a_ref, r):
    s = lax.dot_general(qr, k, (((1,), (1,)), ((), ())), preferred_element_type=F32)
    s = s * (HEAD_DIM ** -0.5) - slope * dist
    s = jnp.where(mask, s, NEG_INF)
    m_old = m_ref[r]
    m_new = jnp.maximum(m_old, jnp.max(s, axis=-1, keepdims=True))
    alpha = jnp.exp(m_old - m_new)
    e = jnp.where(mask, jnp.exp(s - m_new), 0.0)
    l_ref[r] = alpha * l_ref[r] + jnp.sum(e, axis=-1, keepdims=True)
    a_ref[r] = alpha * a_ref[r] + jnp.dot(e.astype(BF16), v, preferred_element_type=F32)
    m_ref[r] = m_new


def _nsa_prompt_kernel(slopes_ref, q_ref, g_ref, kc_ref, vc_ref, ks_ref, vs_ref, kw_ref, vw_ref,
                       ov_ref, ex_ref, o_ref,
                       sel_ref, oc_ref, ms_ref, ls_ref, as_ref, mw_ref, lw_ref, aw_ref, *, n_cmp):
    g = pl.program_id(1)
    qi = pl.program_id(2)
    kt = pl.program_id(3)
    tq = q_ref.shape[0]
    tk = ks_ref.shape[0]
    nsb = ov_ref.shape[1]
    q0 = qi * tq
    last = q0 // tk
    pos_q = q0 + lax.broadcasted_iota(jnp.int32, (tq, 1), 0)

    @pl.when(kt == 0)
    def _():
        for ref in (ms_ref, mw_ref):
            ref[...] = jnp.full(ref.shape, NEG_INF, F32)
        for ref in (ls_ref, lw_ref, as_ref, aw_ref):
            ref[...] = jnp.zeros(ref.shape, F32)
        cidx = lax.broadcasted_iota(jnp.int32, (1, NSA_NC_PAD), 1)
        dc = (pos_q - (cidx * CMP_STRIDE + CMP_BLOCK - 1)).astype(F32)
        mc = jnp.logical_and(dc >= 0, cidx < n_cmp)
        kc = kc_ref[0, 0]
        vc = vc_ref[0, 0]
        ov = ov_ref[...]
        imp = jnp.zeros((tq, nsb), F32)
        for r in range(HEADS_PER_KV):
            qr = q_ref[:, r * HEAD_DIM:(r + 1) * HEAD_DIM]
            s = lax.dot_general(qr, kc, (((1,), (1,)), ((), ())), preferred_element_type=F32)
            s = s * (HEAD_DIM ** -0.5) - slopes_ref[g * HEADS_PER_KV + r] * dc
            s = jnp.where(mc, s, NEG_INF)
            e = jnp.where(mc, jnp.exp(s - jnp.max(s, axis=-1, keepdims=True)), 0.0)
            d = jnp.sum(e, axis=-1, keepdims=True)
            pb = (e / jnp.where(d > 0, d, 1.0)).astype(BF16)
            oc_ref[r] = jnp.dot(pb, vc, preferred_element_type=F32)
            imp = imp + jnp.dot(pb, ov, preferred_element_type=F32)
        jj = lax.broadcasted_iota(jnp.int32, (1, nsb), 1)
        jt = pos_q // SEL_BLOCK
        forced = jnp.logical_or(jj == 0, jnp.logical_or(jj == jt, jj == jt - 1))
        imp = jnp.where(forced, FORCE, jnp.where(jj <= jt, imp, -FORCE))
        sel_ref[...] = jnp.where(_rank_desc(imp, jj) < N_SEL, 1.0, 0.0).astype(BF16)

    @pl.when(kt <= last)
    def _():
        kpos = kt * tk + lax.broadcasted_iota(jnp.int32, (1, tk), 1)
        dist = (pos_q - kpos).astype(F32)
        selk = jnp.dot(sel_ref[...], ex_ref[...], preferred_element_type=F32)
        msel = jnp.logical_and(selk > 0.5, dist >= 0)
        ks = ks_ref[...].astype(BF16)
        vs = vs_ref[...].astype(BF16)
        for r in range(HEADS_PER_KV):
            qr = q_ref[:, r * HEAD_DIM:(r + 1) * HEAD_DIM]
            _flash_update(qr, ks, vs, msel, dist, slopes_ref[g * HEADS_PER_KV + r], ms_ref, ls_ref, as_ref, r)

        @pl.when(kt >= last - 1)
        def _():
            mwin = jnp.logical_and(dist >= 0, dist <= WINDOW)
            kw = kw_ref[...].astype(BF16)
            vw = vw_ref[...].astype(BF16)
            for r in range(HEADS_PER_KV):
                qr = q_ref[:, r * HEAD_DIM:(r + 1) * HEAD_DIM]
                _flash_update(qr, kw, vw, mwin, dist, slopes_ref[g * HEADS_PER_KV + r], mw_ref, lw_ref, aw_ref, r)

    @pl.when(kt == last)
    def _():
        gate = jax.nn.sigmoid(g_ref[...])
        for r in range(HEADS_PER_KV):
            ls = ls_ref[r]
            lw = lw_ref[r]
            o = (gate[:, 3 * r:3 * r + 1] * oc_ref[r]
                 + gate[:, 3 * r + 1:3 * r + 2] * (as_ref[r] / jnp.where(ls > 0, ls, 1.0))
                 + gate[:, 3 * r + 2:3 * r + 3] * (aw_ref[r] / jnp.where(lw > 0, lw, 1.0)))
            o_ref[:, r * HEAD_DIM:(r + 1) * HEAD_DIM] = o.astype(o_ref.dtype)


def _alibi_slopes():
    hh = jnp.arange(1, N_HEADS + 1, dtype=F32)
    return 2.0 ** (-8.0 * hh / N_HEADS)


def _nsa_prompt(q, glog, kv2d, kc, vc, B, T):
    G, R, DH = N_KV_HEADS, HEADS_PER_KV, HEAD_DIM
    tq, tk = NSA_TQ, NSA_TK
    nq, nkt = T // tq, T // tk
    nsb = T // SEL_BLOCK
    n_cmp = T // CMP_STRIDE - CMP_BLOCK // CMP_STRIDE + 1
    ci = jnp.arange(NSA_NC_PAD)[:, None] * CMP_STRIDE
    sj = jnp.arange(nsb)[None, :] * SEL_BLOCK
    overlap = ((ci < sj + SEL_BLOCK) & (ci + CMP_BLOCK > sj) & (jnp.arange(NSA_NC_PAD)[:, None] < n_cmp)).astype(BF16)
    expand = (jnp.arange(nsb)[:, None] == jnp.arange(T)[None, :] // SEL_BLOCK).astype(BF16)

    def last_tile(qi):
        return (qi * tq) // tk

    def sel_rows(b, g, qi, kt):
        return b * nkt + jnp.minimum(kt, last_tile(qi))

    def win_rows(b, g, qi, kt):
        last = last_tile(qi)
        return b * nkt + jnp.clip(kt, jnp.maximum(last - 1, 0), last)

    def kv_spec(rows, branch, kv):
        col0 = (branch * 2 + kv) * G
        return pl.BlockSpec((tk, DH), lambda b, g, qi, kt: (rows(b, g, qi, kt), col0 + g))

    return pl.pallas_call(
        functools.partial(_nsa_prompt_kernel, n_cmp=n_cmp),
        grid=(B, G, nq, nkt),
        in_specs=[
            pl.BlockSpec(memory_space=pltpu.SMEM),
            pl.BlockSpec((tq, R * DH), lambda b, g, qi, kt: (b * nq + qi, g)),
            pl.BlockSpec((None, tq, R * N_BRANCH), lambda b, g, qi, kt: (g, b * nq + qi, 0)),
            pl.BlockSpec((1, 1, NSA_NC_PAD, DH), lambda b, g, qi, kt: (b, g, 0, 0)),
            pl.BlockSpec((1, 1, NSA_NC_PAD, DH), lambda b, g, qi, kt: (b, g, 0, 0)),
            kv_spec(sel_rows, 1, 0), kv_spec(sel_rows, 1, 1),
            kv_spec(win_rows, 2, 0), kv_spec(win_rows, 2, 1),
            pl.BlockSpec((NSA_NC_PAD, nsb), lambda b, g, qi, kt: (0, 0)),
            pl.BlockSpec((nsb, tk), lambda b, g, qi, kt: (0, jnp.minimum(kt, last_tile(qi)))),
        ],
        out_specs=pl.BlockSpec((tq, R * DH), lambda b, g, qi, kt: (b * nq + qi, g)),
        out_shape=jax.ShapeDtypeStruct((B * T, N_HEADS * DH), BF16),
        scratch_shapes=[
            pltpu.VMEM((tq, nsb), BF16),
            pltpu.VMEM((R, tq, DH), F32),
            pltpu.VMEM((R, tq, 1), F32), pltpu.VMEM((R, tq, 1), F32), pltpu.VMEM((R, tq, DH), F32),
            pltpu.VMEM((R, tq, 1), F32), pltpu.VMEM((R, tq, 1), F32), pltpu.VMEM((R, tq, DH), F32),
        ],
        compiler_params=pltpu.CompilerParams(
            dimension_semantics=("parallel", "parallel", "parallel", "arbitrary"),
            vmem_limit_bytes=VMEM_LIMIT_BYTES),
        name="nsa_prompt",
    )(_alibi_slopes(), q, glog, kc, vc, kv2d, kv2d, kv2d, kv2d, overlap, expand)


CMP_PAGES_PER_STEP = 32
SMP_NSB_PAD = 384


def _cmp_sample_kernel(pt_ref, cache_ref, pe_ref, w1_ref, b1_ref, w2_ref, kc_ref, vc_ref, buf, sem, *, n_pages):
    b = pl.program_id(0)
    c = pl.program_id(1)
    pps = CMP_PAGES_PER_STEP
    nseg = pps * (PAGE_SIZE // CMP_STRIDE)

    def page_copy(slot):
        page = pt_ref[b * n_pages + jnp.minimum(c * pps + slot, n_pages - 1)]
        return [pltpu.make_async_copy(cache_ref.at[page, :, cg, :],
                                      buf.at[cg, pl.ds(slot * PAGE_SIZE, PAGE_SIZE), :], sem)
                for cg in range(2 * N_KV_HEADS)]

    for slot in range(pps + 1):
        for cp in page_copy(slot):
            cp.start()
    for slot in range(pps + 1):
        for cp in page_copy(slot):
            cp.wait()
    for kv in range(2):
        out_ref = kc_ref if kv == 0 else vc_ref
        for g in range(N_KV_HEADS):
            cg = kv * N_KV_HEADS + g
            acc = jnp.zeros((nseg, HEAD_DIM), F32)
            for r in range(CMP_BLOCK):
                x = buf[cg, pl.ds(r, nseg, stride=CMP_STRIDE), :] + pe_ref[kv, r:r + 1, :]
                acc = acc + jnp.dot(x.astype(BF16), w1_ref[kv, r], preferred_element_type=F32)
            hid = jax.nn.gelu(acc + b1_ref[kv])
            out = jnp.dot(hid.astype(BF16), w2_ref[kv], preferred_element_type=F32)
            out_ref[0, g] = out.astype(out_ref.dtype)


def _compress_sample(cache_cmp, page_table, pe, w1, b1, w2):
    n_pool = cache_cmp.shape[0]
    B, n_pages = page_table.shape
    G, DH = N_KV_HEADS, HEAD_DIM
    pps = CMP_PAGES_PER_STEP
    nseg = pps * (PAGE_SIZE // CMP_STRIDE)
    cache2 = cache_cmp.reshape(n_pool, PAGE_SIZE, 2 * G, DH)
    w1b = w1.reshape(2, CMP_BLOCK, DH, DH).astype(BF16)
    out_sds = jax.ShapeDtypeStruct((B, G, n_pages * (PAGE_SIZE // CMP_STRIDE), DH), BF16)
    out_spec = pl.BlockSpec((1, G, nseg, DH), lambda b, c, pt: (b, 0, c, 0))
    full = lambda shape: pl.BlockSpec(shape, lambda b, c, pt: (0,) * len(shape))
    return pl.pallas_call(
        functools.partial(_cmp_sample_kernel, n_pages=n_pages),
        grid_spec=pltpu.PrefetchScalarGridSpec(
            num_scalar_prefetch=1,
            grid=(B, n_pages // pps),
            in_specs=[
                pl.BlockSpec(memory_space=pl.ANY),
                full((2, CMP_BLOCK, DH)), full((2, CMP_BLOCK, DH, DH)), full((2, 1, DH)), full((2, DH, DH)),
            ],
            out_specs=[out_spec, out_spec],
            scratch_shapes=[pltpu.VMEM((2 * G, (pps + 1) * PAGE_SIZE, DH), F32), pltpu.SemaphoreType.DMA(())],
        ),
        out_shape=[out_sds, out_sds],
        compiler_params=pltpu.CompilerParams(
            dimension_semantics=("arbitrary", "arbitrary"), vmem_limit_bytes=VMEM_LIMIT_BYTES),
        name="cmp_sample",
    )(page_table.reshape(-1).astype(jnp.int32), cache2, pe, w1b, b1.reshape(2, 1, DH), w2.astype(BF16))


def _smp_select_kernel(q_ref, slope_ref, kc_ref, vc_ref, ov_ref, oc_ref, top_ref, *, n_cmp, pos):
    q = q_ref[0].astype(BF16)
    ncp = kc_ref.shape[2]
    nsbp = ov_ref.shape[1]
    cidx = lax.broadcasted_iota(jnp.int32, (1, ncp), 1)
    dc = (pos - (cidx * CMP_STRIDE + CMP_BLOCK - 1)).astype(F32)
    mc = jnp.logical_and(dc >= 0, cidx < n_cmp)
    s = lax.dot_general(q, kc_ref[0, 0], (((1,), (1,)), ((), ())), preferred_element_type=F32)
    s = s * (HEAD_DIM ** -0.5) - slope_ref[0] * dc
    s = jnp.where(mc, s, NEG_INF)
    e = jnp.where(mc, jnp.exp(s - jnp.max(s, axis=-1, keepdims=True)), 0.0)
    d = jnp.sum(e, axis=-1, keepdims=True)
    pb = (e / jnp.where(d > 0, d, 1.0)).astype(BF16)
    oc_ref[0] = jnp.dot(pb, vc_ref[0, 0], preferred_element_type=F32)
    imp = jnp.sum(jnp.dot(pb, ov_ref[...], preferred_element_type=F32), axis=0, keepdims=True)
    jj = lax.broadcasted_iota(jnp.int32, (1, nsbp), 1)
    jj_f = jj.astype(F32)
    jt = pos // SEL_BLOCK
    forced = jnp.logical_or(jj == 0, jnp.logical_or(jj == jt, jj == jt - 1))
    imp = jnp.where(forced, FORCE, jnp.where(jj <= jt, imp, -FORCE))
    lane = lax.broadcasted_iota(jnp.int32, (1, LANES), 1)
    top = jnp.zeros((1, LANES), F32)
    for k in range(N_SEL):
        best = jnp.max(imp, axis=-1, keepdims=True)
        idx = jnp.min(jnp.where(imp == best, jj_f, 1e9), axis=-1, keepdims=True)
        top = jnp.where(lane == k, idx, top)
        imp = jnp.where(jj_f == idx, -3e38, imp)
    top_ref[0, 0] = top.astype(jnp.int32)


def _smp_attend_kernel(top_ref, pt_ref, q_ref, slope_ref, glog_ref, oc_ref, ks_ref, vs_ref, ksn_ref, vsn_ref,
                       kw_ref, vw_ref, kwn_ref, vwn_ref, o_ref, m_ref, l_ref, a_ref, ow_ref, *, pos):
    b = pl.program_id(0)
    g = pl.program_id(1)
    k = pl.program_id(2)
    scale = HEAD_DIM ** -0.5
    q = q_ref[0].astype(BF16)
    qf = q.astype(F32)
    slope = slope_ref[0]
    jt = pos // SEL_BLOCK

    def new_row(ref):
        return ref[pl.ds(b, 1), :].astype(BF16).astype(F32)

    @pl.when(k == 0)
    def _():
        nw = kw_ref.shape[1]
        dist = (nw - lax.broadcasted_iota(jnp.int32, (1, nw), 1)).astype(F32)
        s = lax.dot_general(q, kw_ref[0].astype(BF16), (((1,), (1,)), ((), ())), preferred_element_type=F32)
        s = s * scale - slope * dist
        s_new = jnp.sum(qf * new_row(kwn_ref), axis=-1, keepdims=True) * scale
        m = jnp.maximum(jnp.max(s, axis=-1, keepdims=True), s_new)
        e = jnp.exp(s - m)
        e_new = jnp.exp(s_new - m)
        d = jnp.sum(e, axis=-1, keepdims=True) + e_new
        ow = jnp.dot((e / d).astype(BF16), vw_ref[0].astype(BF16), preferred_element_type=F32)
        ow_ref[...] = ow + (e_new / d).astype(BF16).astype(F32) * new_row(vwn_ref)
        m_ref[...] = jnp.sum(qf * new_row(ksn_ref), axis=-1, keepdims=True) * scale
        l_ref[...] = jnp.ones(l_ref.shape, F32)
        a_ref[...] = jnp.broadcast_to(new_row(vsn_ref), a_ref.shape)

    top = top_ref[(b * N_KV_HEADS + g) * LANES + k]

    @pl.when(top != jt)
    def _():
        kpos = top * SEL_BLOCK + lax.broadcasted_iota(jnp.int32, (1, SEL_BLOCK), 1)
        dist = (pos - kpos).astype(F32)
        s = lax.dot_general(q, ks_ref[0].astype(BF16), (((1,), (1,)), ((), ())), preferred_element_type=F32)
        s = s * scale - slope * dist
        m_old = m_ref[...]
        m_new = jnp.maximum(m_old, jnp.max(s, axis=-1, keepdims=True))
        alpha = jnp.exp(m_old - m_new)
        e = jnp.exp(s - m_new)
        l_ref[...] = alpha * l_ref[...] + jnp.sum(e, axis=-1, keepdims=True)
        a_ref[...] = alpha * a_ref[...] + jnp.dot(e.astype(BF16), vs_ref[0].astype(BF16), preferred_element_type=F32)
        m_ref[...] = m_new

    @pl.when(k == N_SEL - 1)
    def _():
        gate = jax.nn.sigmoid(glog_ref[0, 0])
        o_ref[0] = (gate[:, 0:1] * oc_ref[0] + gate[:, 1:2] * (a_ref[...] / l_ref[...])
                    + gate[:, 2:3] * ow_ref[...])


def _nsa_sample(q, glog, kv2d, row0, kc, vc, cache_slc, state_win, page_table):
    B, n_pages = page_table.shape
    G, R, DH = N_KV_HEADS, HEADS_PER_KV, HEAD_DIM
    pos = n_pages * PAGE_SIZE
    ncp = kc.shape[2]
    n_cmp = (pos + 1) // CMP_STRIDE - CMP_BLOCK // CMP_STRIDE + 1
    nsb = _cdiv(pos + 1, SEL_BLOCK)
    assert nsb <= SMP_NSB_PAD and n_cmp <= ncp and row0 % 8 == 0 and B <= 8
    slopes = _alibi_slopes().reshape(G, R, 1)
    ci = jnp.arange(ncp)[:, None] * CMP_STRIDE
    sj = jnp.arange(SMP_NSB_PAD)[None, :] * SEL_BLOCK
    overlap = ((ci < sj + SEL_BLOCK) & (ci + CMP_BLOCK > sj) & (jnp.arange(ncp)[:, None] < n_cmp)).astype(BF16)
    head_spec = lambda *_: None
    oc, top = pl.pallas_call(
        functools.partial(_smp_select_kernel, n_cmp=n_cmp, pos=pos),
        grid=(B, G),
        in_specs=[
            pl.BlockSpec((1, R, DH), lambda b, g: (b, g, 0)),
            pl.BlockSpec((1, R, 1), lambda b, g: (g, 0, 0)),
            pl.BlockSpec((1, 1, ncp, DH), lambda b, g: (b, g, 0, 0)),
            pl.BlockSpec((1, 1, ncp, DH), lambda b, g: (b, g, 0, 0)),
            pl.BlockSpec((ncp, SMP_NSB_PAD), lambda b, g: (0, 0)),
        ],
        out_specs=[pl.BlockSpec((1, R, DH), lambda b, g: (b, g, 0)),
                   pl.BlockSpec((1, 1, 1, LANES), lambda b, g: (b, g, 0, 0))],
        out_shape=[jax.ShapeDtypeStruct((B, G * R, DH), F32), jax.ShapeDtypeStruct((B, G, 1, LANES), jnp.int32)],
        compiler_params=pltpu.CompilerParams(
            dimension_semantics=("parallel", "parallel"), vmem_limit_bytes=VMEM_LIMIT_BYTES),
        name="smp_select",
    )(q, slopes, kc, vc, overlap)

    n_pool = cache_slc.shape[0]
    halves = PAGE_SIZE // SEL_BLOCK
    slc2 = cache_slc.reshape(n_pool * halves, SEL_BLOCK, 2 * G * DH)
    win2 = state_win.reshape(B, state_win.shape[1], 2 * G * DH)
    jt = pos // SEL_BLOCK

    def blk(b, g, k, top_ref, pt_ref):
        j = jnp.minimum(top_ref[(b * G + g) * LANES + k], jt - 1)
        return pt_ref[b * n_pages + j // halves] * halves + j % halves

    def new_spec(branch, kv):
        col0 = (branch * 2 + kv) * G
        return pl.BlockSpec((8, DH), lambda b, g, k, t, p: (row0 // 8, col0 + g))

    return pl.pallas_call(
        functools.partial(_smp_attend_kernel, pos=pos),
        grid_spec=pltpu.PrefetchScalarGridSpec(
            num_scalar_prefetch=2,
            grid=(B, G, N_SEL),
            in_specs=[
                pl.BlockSpec((1, R, DH), lambda b, g, k, t, p: (b, g, 0)),
                pl.BlockSpec((1, R, 1), lambda b, g, k, t, p: (g, 0, 0)),
                pl.BlockSpec((1, 1, R, N_BRANCH), lambda b, g, k, t, p: (b, g, 0, 0)),
                pl.BlockSpec((1, R, DH), lambda b, g, k, t, p: (b, g, 0)),
                pl.BlockSpec((1, SEL_BLOCK, DH), lambda b, g, k, t, p: (blk(b, g, k, t, p), 0, g)),
                pl.BlockSpec((1, SEL_BLOCK, DH), lambda b, g, k, t, p: (blk(b, g, k, t, p), 0, G + g)),
                new_spec(1, 0), new_spec(1, 1),
                pl.BlockSpec((1, win2.shape[1], DH), lambda b, g, k, t, p: (b, 0, g)),
                pl.BlockSpec((1, win2.shape[1], DH), lambda b, g, k, t, p: (b, 0, G + g)),
                new_spec(2, 0), new_spec(2, 1),
            ],
            out_specs=pl.BlockSpec((1, R, DH), lambda b, g, k, t, p: (b, g, 0)),
            scratch_shapes=[pltpu.VMEM((R, 1), F32), pltpu.VMEM((R, 1), F32), pltpu.VMEM((R, DH), F32),
                            pltpu.VMEM((R, DH), F32)],
        ),
        out_shape=jax.ShapeDtypeStruct((B, G * R, DH), F32),
        compiler_params=pltpu.CompilerParams(
            dimension_semantics=("arbitrary", "arbitrary", "arbitrary"), vmem_limit_bytes=VMEM_LIMIT_BYTES),
        name="smp_attend",
    )(top.reshape(-1), page_table.reshape(-1).astype(jnp.int32), q, slopes, glog, oc, slc2, slc2, kv2d, kv2d,
      win2, win2, kv2d, kv2d)


def _nsa_mixer(hn, kv2d, p, j, n_prompt_seq, seq_len, cache_cmp, cache_slc, state_win, page_table):
    N, D = hn.shape
    HD = N_HEADS * HEAD_DIM
    G, R = N_KV_HEADS, HEADS_PER_KV
    B, T = n_prompt_seq, seq_len
    Np = B * T
    Bs = N - Np
    w_qg = p['w_qg'][j]
    q = _mm(hn, w_qg[None, :, :HD], 0, out_dtype=BF16)
    n_gate = N_BRANCH * N_HEADS
    glog = _mm(hn, jnp.pad(w_qg[None, :, HD:], ((0, 0), (0, 0), (0, LANES - n_gate))), 0)[:, :n_gate]
    kv_p = kv2d[:Np].reshape(B, T, N_BRANCH, 2, G, HEAD_DIM)
    kc, vc = (_compress(kv_p[:, :, 0, i], p['pe_cmp'][i], p['w_cmp1'][i], p['b_cmp1'][i], p['w_cmp2'][i])
              for i in range(2))
    pad = NSA_NC_PAD - kc.shape[1]
    kcp = jnp.pad(kc, ((0, 0), (0, pad), (0, 0), (0, 0))).transpose(0, 2, 1, 3).astype(BF16)
    vcp = jnp.pad(vc, ((0, 0), (0, pad), (0, 0), (0, 0))).transpose(0, 2, 1, 3).astype(BF16)
    o_p = _nsa_prompt(q, glog[:Np].reshape(Np, G, R * N_BRANCH).transpose(1, 0, 2), kv2d, kcp, vcp, B, T)
    kcs, vcs = _compress_sample(cache_cmp, page_table, p['pe_cmp'], p['w_cmp1'], p['b_cmp1'], p['w_cmp2'])
    o_s = _nsa_sample(q[Np:].astype(F32).reshape(Bs, N_HEADS, HEAD_DIM), glog[Np:].reshape(Bs, G, R, N_BRANCH),
                      kv2d, Np, kcs, vcs, cache_slc, state_win, page_table)
    o = jnp.concatenate([o_p, o_s.reshape(Bs, HD).astype(BF16)], axis=0)
    return _mm(o, p['w_o'], j)


def kernel(x_prompt, x_sample, c_prompt, c_sample, cache_cmp_kv, cache_slc_kv, state_win_kv, page_table,
           w_ada, b_ada, g_norm_mix, g_norm_ffn, w_gmlp_in, b_gmlp_in, g_sgu_ln, b_sgu_ln, w_sgu, b_sgu,
           w_gmlp_out, w_ada_kv, b_ada_kv, g_norm_kv, w_kv, pe_cmp, w_cmp1, b_cmp1, w_cmp2, w_qg, w_o,
           w_router, b_router, w_exp1, w_exp3, w_exp2, w_sh1, w_sh3, w_sh2, g_final):
    p = dict(w_gmlp_in=w_gmlp_in, b_gmlp_in=b_gmlp_in, g_sgu_ln=g_sgu_ln, b_sgu_ln=b_sgu_ln,
             w_sgu=w_sgu, b_sgu=b_sgu, w_gmlp_out=w_gmlp_out, pe_cmp=pe_cmp, w_cmp1=w_cmp1, b_cmp1=b_cmp1,
             w_cmp2=w_cmp2, w_qg=w_qg, w_o=w_o, w_router=w_router, b_router=b_router, w_exp1=w_exp1,
             w_exp3=w_exp3, w_exp2=w_exp2, w_sh1=w_sh1, w_sh3=w_sh3, w_sh2=w_sh2)
    B, T, D = x_prompt.shape
    Bs = x_sample.shape[0]
    Np = B * T
    assert x_sample.shape[1] == 1 and T % ROW_TILE == 0 and Bs <= ROW_TILE
    h = jnp.concatenate([x_prompt.reshape(Np, D), x_sample.reshape(Bs, D)], axis=0)
    c_act = jax.nn.silu(jnp.concatenate([c_prompt, c_sample], axis=0))
    resnorm = functools.partial(_resnorm, n_prompt_seq=B, seq_len=T)
    zeros_tab = jnp.zeros((B + Bs, D), F32)
    v_rows, kv2d = None, None
    pending, gate = [], None
    for l in range(DEPTH):
        mod = _mm(c_act, w_ada, l, bias=b_ada[l])
        sh1, sc1, gt1, sh2, sc2, gt2 = jnp.split(mod, 6, axis=-1)
        if pending:
            h, hn = resnorm(h, pending, gate, g_norm_mix[l], sc1, sh1)
        else:
            hn = resnorm(h, [], None, g_norm_mix[l], sc1, sh1)
        if l < N_A_LAYERS:
            mix, v_rows = _gmlp_mixer(hn, p, l, B, T)
        else:
            if kv2d is None:
                mkv = _mm(c_act, w_ada_kv[None], 0, bias=b_ada_kv)
                sh_kv, sc_kv = jnp.split(mkv, 2, axis=-1)
                kvn = resnorm(h, [], None, g_norm_kv, sc_kv, sh_kv)
                kv2d = _mm(kvn, w_kv[None], 0)
            mix = _nsa_mixer(hn, kv2d, p, l - N_A_LAYERS, B, T, cache_cmp_kv, cache_slc_kv, state_win_kv, page_table)
        h, hn, logits = resnorm(h, [mix], gt1, g_norm_ffn[l], sc2, sh2, w_router=w_router[l])
        pending, gate = list(_moe_ffn(hn, logits, p, l)), gt2
    y = resnorm(h, pending, gate, g_final, zeros_tab, zeros_tab, out_dtype=F32, write_h=False)
    G, DH = N_KV_HEADS, HEAD_DIM
    y_prompt = y[:Np].reshape(B, T, D)
    y_sample = y[Np:].reshape(Bs, 1, D)
    v_p = v_rows[:Np].reshape(B, T, GMLP_DFF)[:, T - CHUNK:][None]
    v_s = v_rows[Np:].reshape(Bs, 1, GMLP_DFF)[None]
    kv_p = kv2d[:Np].reshape(B, T, N_BRANCH, 2, G, DH)
    kv_s = kv2d[Np:].reshape(Bs, 1, N_BRANCH, 2, G, DH)
    n_pg = T // PAGE_SIZE
    cmp_kv_prompt = kv_p[:, :, 0].reshape(B, n_pg, PAGE_SIZE, 2, G, DH)
    slc_kv_prompt = kv_p[:, :, 1].reshape(B, n_pg, PAGE_SIZE, 2, G, DH)
    win_kv_prompt = kv_p[:, T - min(WINDOW, T):, 2]
    return (y_prompt, y_sample, v_p, v_s, cmp_kv_prompt, kv_s[:, :, 0], slc_kv_prompt, kv_s[:, :, 1],
            win_kv_prompt, kv_s[:, :, 2])
```

```python
import functools

import jax
import jax.numpy as jnp
from jax import lax
from jax.experimental import pallas as pl
from jax.experimental.pallas import tpu as pltpu

D_MODEL = 4096
DEPTH = 2
PAGE_SIZE = 128
N_A_LAYERS = DEPTH // 2
CHUNK = 128
GMLP_DFF = 2 * D_MODEL
GMLP_GROUPS = 16
GMLP_GDIM = GMLP_DFF // GMLP_GROUPS
N_HEADS = 32
HEAD_DIM = D_MODEL // N_HEADS
N_KV_HEADS = 4
HEADS_PER_KV = N_HEADS // N_KV_HEADS
N_BRANCH = 3
CMP_BLOCK = 32
CMP_STRIDE = 16
SEL_BLOCK = 64
N_SEL = 16
WINDOW = 512
N_EXPERTS = 64
TOP_K = 8
N_GROUPS = 8
TOPK_GROUPS = 4
EXPERT_DFF = D_MODEL // 4
ROUTE_SCALE = 2.5
RMS_EPS = 1e-6
LN_EPS = 1e-5
NEG_INF = -1e30
FORCE = 1e30

BF16 = jnp.bfloat16
F32 = jnp.float32

VMEM_LIMIT_BYTES = 56 * 1024 * 1024
BF16_SUBLANES = 16
LANES = 128
MOE_FF_CHUNK = 256
MOE_OUT_CHUNK = 1024
MOE_ROWS = 512
ROW_TILE = 128


def _cdiv(a, b):
    return -(-a // b)


def _round_up(a, b):
    return _cdiv(a, b) * b


def _mm_kernel(x_ref, w_ref, b_ref, o_ref, *, act):
    acc = jnp.dot(x_ref[...], w_ref[0].astype(BF16), preferred_element_type=F32)
    acc = acc + b_ref[...]
    if act == "gelu":
        acc = jax.nn.gelu(acc)
    o_ref[...] = acc.astype(o_ref.dtype)


def _mm(x, w, layer=0, bias=None, act=None, out_dtype=F32):
    M, K = x.shape
    N = w.shape[2]
    assert N % LANES == 0
    xb = x.astype(BF16)
    if M >= 1024:
        n_tiles = _cdiv(M, 1024 if K <= 4096 else 512)
        tm = _round_up(_cdiv(M, n_tiles), BF16_SUBLANES)
        rows = M
    else:
        tm = _round_up(M, BF16_SUBLANES)
        rows = tm
        xb = jnp.pad(xb, ((0, rows - M), (0, 0)))
    if _cdiv(rows, tm) > 1:
        w = w[layer:layer + 1].astype(BF16)
        layer = 0
    tn = 512 if N % 512 == 0 else (256 if N % 256 == 0 else 128)
    b2 = jnp.zeros((1, N), F32) if bias is None else bias.reshape(1, N).astype(F32)
    out = pl.pallas_call(
        functools.partial(_mm_kernel, act=act),
        grid=(_cdiv(rows, tm), N // tn),
        in_specs=[
            pl.BlockSpec((tm, K), lambda i, j: (i, 0)),
            pl.BlockSpec((1, K, tn), lambda i, j: (layer, 0, j)),
            pl.BlockSpec((1, tn), lambda i, j: (0, j)),
        ],
        out_specs=pl.BlockSpec((tm, tn), lambda i, j: (i, j)),
        out_shape=jax.ShapeDtypeStruct((rows, N), out_dtype),
        compiler_params=pltpu.CompilerParams(
            dimension_semantics=("parallel", "arbitrary"),
            vmem_limit_bytes=VMEM_LIMIT_BYTES),
        name="dense_mm",
    )(xb, w, b2)
    return out[:M]


def _resnorm_kernel(*refs, n_res, n_prompt_tiles, tiles_per_seq, n_prompt_seq, write_h, has_router):
    h_ref = refs[0]
    res_refs = refs[1:1 + n_res]
    pos = 1 + n_res
    if n_res:
        gt_ref = refs[pos]
        pos += 1
    g_ref, sc_ref, sh_ref = refs[pos:pos + 3]
    pos += 3
    if has_router:
        wr_ref = refs[pos]
        pos += 1
    outs = list(refs[pos:])
    hnew_ref = outs.pop(0) if (n_res and write_h) else None
    hn_ref = outs.pop(0)
    i = pl.program_id(0)
    tm = h_ref.shape[0]

    def body(rows_of):
        h = h_ref[...]
        if n_res:
            r = res_refs[0][...]
            for extra in res_refs[1:]:
                r = r + extra[...]
            h = h + rows_of(gt_ref) * r
            if hnew_ref is not None:
                hnew_ref[...] = h
        y = h * lax.rsqrt(jnp.mean(h * h, axis=-1, keepdims=True) + RMS_EPS) * g_ref[...]
        hn = y * (1.0 + rows_of(sc_ref)) + rows_of(sh_ref)
        hn_ref[...] = hn.astype(hn_ref.dtype)
        if has_router:
            outs[0][...] = jnp.dot(hn, wr_ref[...], preferred_element_type=F32, precision=lax.Precision.HIGHEST)

    @pl.when(i < n_prompt_tiles)
    def _():
        seq = i // tiles_per_seq
        body(lambda tab: tab[pl.ds(seq, 1), :])

    @pl.when(i >= n_prompt_tiles)
    def _():
        body(lambda tab: tab[pl.ds(n_prompt_seq, tm), :])


def _row_table(m, n_prompt_seq):
    n_sample = m.shape[0] - n_prompt_seq
    return jnp.pad(m, ((0, ROW_TILE - n_sample), (0, 0)))


def _resnorm(h, res, gt, g, sc, sh, n_prompt_seq, seq_len, out_dtype=BF16, write_h=True, w_router=None):
    N, D = h.shape
    tm = ROW_TILE
    n_res = len(res)
    has_router = w_router is not None
    tabs = [_row_table(m, n_prompt_seq) for m in (([gt] if n_res else []) + [sc, sh])]
    tab_rows = tabs[0].shape[0]
    row_spec = pl.BlockSpec((tm, D), lambda i: (i, 0))
    tab_spec = pl.BlockSpec((tab_rows, D), lambda i: (0, 0))
    in_specs = [row_spec] * (1 + n_res) + ([tab_spec] if n_res else []) + [pl.BlockSpec((1, D), lambda i: (0, 0))] \
        + [tab_spec, tab_spec]
    operands = [h, *res, *tabs[:1 if n_res else 0], g.reshape(1, D), *tabs[-2:]]
    out_shape = [jax.ShapeDtypeStruct((N, D), out_dtype)]
    out_specs = [row_spec]
    if n_res and write_h:
        out_shape = [jax.ShapeDtypeStruct((N, D), F32)] + out_shape
        out_specs = [row_spec] + out_specs
    if has_router:
        E = w_router.shape[1]
        in_specs.append(pl.BlockSpec((D, E), lambda i: (0, 0)))
        operands.append(w_router)
        out_shape.append(jax.ShapeDtypeStruct((N, E), F32))
        out_specs.append(pl.BlockSpec((tm, E), lambda i: (i, 0)))
    outs = pl.pallas_call(
        functools.partial(_resnorm_kernel, n_res=n_res, n_prompt_tiles=n_prompt_seq * seq_len // tm,
                          tiles_per_seq=seq_len // tm, n_prompt_seq=n_prompt_seq, write_h=write_h,
                          has_router=has_router),
        grid=(_cdiv(N, tm),),
        in_specs=in_specs,
        out_specs=out_specs,
        out_shape=out_shape,
        compiler_params=pltpu.CompilerParams(
            dimension_semantics=("parallel",), vmem_limit_bytes=VMEM_LIMIT_BYTES),
        name="resnorm",
    )(*operands)
    return outs if len(outs) > 1 else outs[0]


def _rank_desc(v, lane, same_group=None):
    n = v.shape[1]
    rank = jnp.zeros(v.shape, F32)
    for j in range(n):
        col = v[:, j:j + 1]
        tie = jnp.where(lane > j, 1.0, 0.0)
        beats = jnp.where(col > v, 1.0, jnp.where(col == v, tie, 0.0))
        if same_group is not None:
            beats = jnp.where(same_group(j), beats, 0.0)
        rank = rank + beats
    return rank


def _router_kernel(logit_ref, b_ref, tri_ref, eidx_ref, gw_ref):
    gs = N_EXPERTS // N_GROUPS
    s = jax.nn.sigmoid(logit_ref[...])
    sel = s + b_ref[...]
    lane = lax.broadcasted_iota(jnp.int32, (1, N_EXPERTS), 1)
    grp = lane // gs
    rank_g = _rank_desc(sel, lane, same_group=lambda j: grp == j // gs)
    top2 = jnp.where(rank_g < 2, sel, 0.0)
    gsum = [jnp.sum(jnp.where(grp == g, top2, 0.0), axis=-1, keepdims=True) for g in range(N_GROUPS)]
    keep = jnp.zeros(sel.shape, F32)
    for g in range(N_GROUPS):
        before = jnp.zeros(gsum[g].shape, F32)
        for g2 in range(N_GROUPS):
            if g2 == g:
                continue
            wins = jnp.where(gsum[g2] > gsum[g], 1.0, jnp.where(gsum[g2] == gsum[g], 1.0 if g2 < g else 0.0, 0.0))
            before = before + wins
        keep = keep + jnp.where(grp == g, jnp.where(before < TOPK_GROUPS, 1.0, 0.0), 0.0)
    masked = jnp.where(keep > 0.5, sel, NEG_INF)
    chosen = _rank_desc(masked, lane) < TOP_K
    sc = jnp.where(chosen, s, 0.0)
    gd = sc / jnp.sum(sc, axis=-1, keepdims=True) * ROUTE_SCALE
    slot = jnp.dot(jnp.where(chosen, 1.0, 0.0).astype(BF16), tri_ref[...], preferred_element_type=F32)
    lane_f = lane.astype(F32)
    kcol = lax.broadcasted_iota(jnp.int32, (1, TOP_K), 1)
    eidx = jnp.zeros((sel.shape[0], TOP_K), F32)
    gw = jnp.zeros((sel.shape[0], TOP_K), F32)
    for k in range(TOP_K):
        hit = jnp.where(chosen, jnp.where(slot == k, 1.0, 0.0), 0.0)
        eidx = jnp.where(kcol == k, jnp.sum(hit * lane_f, axis=-1, keepdims=True), eidx)
        gw = jnp.where(kcol == k, jnp.sum(hit * gd, axis=-1, keepdims=True), gw)
    eidx_ref[...] = eidx.astype(jnp.int32)
    gw_ref[...] = gw


def _route(logits, b_router, layer):
    N = logits.shape[0]
    tm = ROW_TILE
    tri = (jnp.arange(N_EXPERTS)[:, None] < jnp.arange(N_EXPERTS)[None, :]).astype(BF16)
    return pl.pallas_call(
        _router_kernel,
        grid=(_cdiv(N, tm),),
        in_specs=[
            pl.BlockSpec((tm, N_EXPERTS), lambda i: (i, 0)),
            pl.BlockSpec((1, N_EXPERTS), lambda i: (0, 0)),
            pl.BlockSpec((N_EXPERTS, N_EXPERTS), lambda i: (0, 0)),
        ],
        out_specs=[pl.BlockSpec((tm, TOP_K), lambda i: (i, 0)), pl.BlockSpec((tm, TOP_K), lambda i: (i, 0))],
        out_shape=[jax.ShapeDtypeStruct((N, TOP_K), jnp.int32), jax.ShapeDtypeStruct((N, TOP_K), F32)],
        compiler_params=pltpu.CompilerParams(
            dimension_semantics=("parallel",), vmem_limit_bytes=VMEM_LIMIT_BYTES),
        name="router",
    )(logits, b_router[layer].reshape(1, N_EXPERTS).astype(F32), tri)


def _is_new_expert(blk_e_ref, i):
    prev = blk_e_ref[jnp.maximum(i - 1, 0)]
    return jnp.logical_or(i == 0, blk_e_ref[i] != prev)


def _moe_up_kernel(blk_e_ref, nused_ref, x_ref, w1_ref, w3_ref, h_ref, wb_ref):
    i = pl.program_id(1)
    valid = i < nused_ref[0]

    @pl.when(valid)
    def _():
        @pl.when(_is_new_expert(blk_e_ref, i))
        def _():
            wb_ref[0] = w1_ref[0, 0].astype(BF16)
            wb_ref[1] = w3_ref[0, 0].astype(BF16)

        x = x_ref[...]
        a = jnp.dot(x, wb_ref[0], preferred_element_type=F32)
        g = jnp.dot(x, wb_ref[1], preferred_element_type=F32)
        h_ref[...] = (a * jax.nn.sigmoid(a) * g).astype(h_ref.dtype)

    @pl.when(jnp.logical_not(valid))
    def _():
        h_ref[...] = jnp.zeros_like(h_ref)


def _moe_down_kernel(blk_e_ref, nused_ref, h_ref, g_ref, w2_ref, y_ref, wb_ref):
    i = pl.program_id(1)
    valid = i < nused_ref[0]

    @pl.when(valid)
    def _():
        @pl.when(_is_new_expert(blk_e_ref, i))
        def _():
            wb_ref[...] = w2_ref[0, 0].astype(BF16)

        y = jnp.dot(h_ref[...], wb_ref[...], preferred_element_type=F32)
        y_ref[...] = y * g_ref[...]

    @pl.when(jnp.logical_not(valid))
    def _():
        y_ref[...] = jnp.zeros_like(y_ref)


def _grouped_ffn(xs, row_g, blk_e, nused, w1, w3, w2, layer):
    P, D = xs.shape
    F = w1.shape[3]
    R = MOE_ROWS
    nb = _cdiv(P, R)
    fc = MOE_FF_CHUNK
    oc = MOE_OUT_CHUNK

    def row_map(j, i, blk_e_ref, nused_ref):
        return (jnp.minimum(i, nused_ref[0] - 1), 0)

    def expert_of(i, blk_e_ref, nused_ref):
        return blk_e_ref[jnp.minimum(i, nused_ref[0] - 1)]

    h = pl.pallas_call(
        _moe_up_kernel,
        grid_spec=pltpu.PrefetchScalarGridSpec(
            num_scalar_prefetch=2,
            grid=(F // fc, nb),
            in_specs=[
                pl.BlockSpec((R, D), row_map),
                pl.BlockSpec((1, 1, D, fc), lambda j, i, be, nu: (layer, expert_of(i, be, nu), 0, j)),
                pl.BlockSpec((1, 1, D, fc), lambda j, i, be, nu: (layer, expert_of(i, be, nu), 0, j)),
            ],
            out_specs=pl.BlockSpec((R, fc), lambda j, i, be, nu: (i, j)),
            scratch_shapes=[pltpu.VMEM((2, D, fc), BF16)],
        ),
        out_shape=jax.ShapeDtypeStruct((P, F), BF16),
        compiler_params=pltpu.CompilerParams(
            dimension_semantics=("arbitrary", "arbitrary"),
            vmem_limit_bytes=VMEM_LIMIT_BYTES),
        name="moe_up",
    )(blk_e, nused, xs, w1, w3)

    y = pl.pallas_call(
        _moe_down_kernel,
        grid_spec=pltpu.PrefetchScalarGridSpec(
            num_scalar_prefetch=2,
            grid=(D // oc, nb),
            in_specs=[
                pl.BlockSpec((R, F), row_map),
                pl.BlockSpec((R, 1), row_map),
                pl.BlockSpec((1, 1, F, oc), lambda n, i, be, nu: (layer, expert_of(i, be, nu), 0, n)),
            ],
            out_specs=pl.BlockSpec((R, oc), lambda n, i, be, nu: (i, n)),
            scratch_shapes=[pltpu.VMEM((F, oc), BF16)],
        ),
        out_shape=jax.ShapeDtypeStruct((P, D), F32),
        compiler_params=pltpu.CompilerParams(
            dimension_semantics=("arbitrary", "arbitrary"),
            vmem_limit_bytes=VMEM_LIMIT_BYTES),
        name="moe_down",
    )(blk_e, nused, h, row_g, w2)
    return y


def _routed_experts(xt, eidx, gw, w1, w3, w2, layer):
    N, D = xt.shape
    P = N * TOP_K
    R = MOE_ROWS
    rows = _round_up(P + N_EXPERTS * (R - 1), R)
    nblk = rows // R
    cs = 256
    Pp = _round_up(P, cs)
    flat_e = jnp.pad(eidx.reshape(P).astype(jnp.int32), (0, Pp - P), constant_values=N_EXPERTS)
    onehot = (flat_e[:, None] == jnp.arange(N_EXPERTS, dtype=jnp.int32)[None, :])
    oh3 = onehot.reshape(Pp // cs, cs, N_EXPERTS).astype(BF16)
    tri = (jnp.arange(cs)[:, None] > jnp.arange(cs)[None, :]).astype(BF16)
    inner = jnp.einsum('ts,cse->cte', tri, oh3, preferred_element_type=F32)
    chunk_tot = jnp.sum(oh3.astype(F32), axis=1)
    chunk_off = jnp.cumsum(chunk_tot, axis=0) - chunk_tot
    prefix = (inner + chunk_off[:, None, :]).reshape(Pp, N_EXPERTS)
    rank = jnp.sum(jnp.where(onehot, prefix, 0.0), axis=1).astype(jnp.int32)[:P]
    counts = jnp.sum(chunk_tot, axis=0).astype(jnp.int32)
    padded = (counts + R - 1) // R * R
    pends = jnp.cumsum(padded)
    pstarts = pends - padded
    dest = pstarts[flat_e[:P]] + rank
    pair_tok = jnp.arange(P, dtype=jnp.int32) // TOP_K
    row_tok = jnp.full((rows,), N, dtype=jnp.int32).at[dest].set(pair_tok)
    row_g = jnp.ones((rows,), F32)
    blk_e = jnp.clip(jnp.searchsorted(pends, jnp.arange(nblk, dtype=jnp.int32) * R, side='right'),
                     0, N_EXPERTS - 1).astype(jnp.int32)
    nused = (pends[-1] // R).astype(jnp.int32).reshape(1)
    x_pad = jnp.concatenate([xt, jnp.zeros((1, D), BF16)], axis=0)
    xs = x_pad[row_tok]
    yb = _grouped_ffn(xs, row_g.reshape(rows, 1), blk_e, nused, w1, w3, w2, layer)
    return jnp.sum(yb[dest].reshape(N, TOP_K, D) * gw[:, :, None], axis=1)


def _shared_expert(xt, ws1, ws3, ws2, layer):
    N, D = xt.shape
    nblk = _cdiv(N, MOE_ROWS)
    return _grouped_ffn(xt, jnp.ones((N, 1), F32), jnp.zeros((nblk,), jnp.int32), jnp.full((1,), nblk, jnp.int32),
                        ws1[:, None], ws3[:, None], ws2[:, None], layer)


def _moe_ffn(hn, logits, p, layer):
    eidx, gw = _route(logits, p['b_router'], layer)
    routed = _routed_experts(hn, eidx, gw, p['w_exp1'], p['w_exp3'], p['w_exp2'], layer)
    shared = _shared_expert(hn, p['w_sh1'], p['w_sh3'], p['w_sh2'], layer)
    return routed, shared


def _layernorm(x, g, b):
    xc = x - jnp.mean(x, axis=-1, keepdims=True)
    y = xc * lax.rsqrt(jnp.mean(xc * xc, axis=-1, keepdims=True) + LN_EPS)
    return y * g + b


def _sgu_kernel(u_ref, v_ref, g_ref, b_ref, w_ref, bm_ref, um_ref, vn_ref, *, n_rows):
    tm = v_ref.shape[0]
    v = v_ref[...]
    vc = v - jnp.mean(v, axis=-1, keepdims=True)
    vn = vc * lax.rsqrt(jnp.mean(vc * vc, axis=-1, keepdims=True) + LN_EPS) * g_ref[...] + b_ref[...]
    vn_ref[...] = vn
    row = pl.program_id(0) * tm + lax.broadcasted_iota(jnp.int32, (tm, 1), 0)
    vb = jnp.where(row < n_rows, vn, 0.0).astype(BF16)
    gd = GMLP_GDIM
    for g in range(GMLP_GROUPS):
        mixed = jnp.dot(w_ref[0, g], vb[:, g * gd:(g + 1) * gd], preferred_element_type=F32) + bm_ref[0][:, g:g + 1]
        um_ref[:, g * gd:(g + 1) * gd] = (u_ref[:, g * gd:(g + 1) * gd] * mixed).astype(um_ref.dtype)


def _gmlp_mixer(hn, p, l, n_prompt_seq, seq_len):
    N, D = hn.shape
    n = CHUNK
    assert ROW_TILE == n
    n_prompt_tiles = n_prompt_seq * seq_len // n
    uv = _mm(hn, p['w_gmlp_in'], l, bias=p['b_gmlp_in'][l], act="gelu")
    w_s, b_s = p['w_sgu'][l], p['b_sgu'][l]
    causal = jnp.tril(jnp.ones((n, n), dtype=bool))
    w_mix = jnp.stack([jnp.where(causal, w_s, 0.0), jnp.eye(n, dtype=F32)[None] * w_s[:, 0, 0][:, None, None]]).astype(BF16)
    b_mix = jnp.stack([b_s.T, jnp.broadcast_to(b_s[:, 0][None, :], (n, GMLP_GROUPS))])
    variant = lambda i: jnp.where(i < n_prompt_tiles, 0, 1)
    um, vn = pl.pallas_call(
        functools.partial(_sgu_kernel, n_rows=N),
        grid=(_cdiv(N, n),),
        in_specs=[
            pl.BlockSpec((n, GMLP_DFF), lambda i: (i, 0)),
            pl.BlockSpec((n, GMLP_DFF), lambda i: (i, 1)),
            pl.BlockSpec((1, GMLP_DFF), lambda i: (0, 0)),
            pl.BlockSpec((1, GMLP_DFF), lambda i: (0, 0)),
            pl.BlockSpec((1, GMLP_GROUPS, n, n), lambda i: (variant(i), 0, 0, 0)),
            pl.BlockSpec((1, n, GMLP_GROUPS), lambda i: (variant(i), 0, 0)),
        ],
        out_specs=[pl.BlockSpec((n, GMLP_DFF), lambda i: (i, 0)), pl.BlockSpec((n, GMLP_DFF), lambda i: (i, 0))],
        out_shape=[jax.ShapeDtypeStruct((N, GMLP_DFF), BF16), jax.ShapeDtypeStruct((N, GMLP_DFF), F32)],
        compiler_params=pltpu.CompilerParams(
            dimension_semantics=("parallel",), vmem_limit_bytes=VMEM_LIMIT_BYTES),
        name="sgu",
    )(uv, uv, p['g_sgu_ln'][l].reshape(1, GMLP_DFF), p['b_sgu_ln'][l].reshape(1, GMLP_DFF), w_mix, b_mix)
    return _mm(um, p['w_gmlp_out'], l), vn


def _compress(k, pe, w1, b1, w2):
    B, L = k.shape[:2]
    nseg = L // CMP_STRIDE
    n_sub = CMP_BLOCK // CMP_STRIDE
    nc = nseg - n_sub + 1
    seg = k[:, :nseg * CMP_STRIDE].reshape(B, nseg, CMP_STRIDE, N_KV_HEADS, HEAD_DIM)
    blocks = jnp.concatenate([seg[:, i:i + nc] for i in range(n_sub)], axis=2)
    blocks = blocks + pe[None, None, :, None, :]
    flat = blocks.transpose(0, 1, 3, 2, 4).reshape(B * nc * N_KV_HEADS, CMP_BLOCK * HEAD_DIM)
    hid = _mm(flat, w1[None], 0, bias=b1, act="gelu")
    return _mm(hid, w2[None], 0).reshape(B, nc, N_KV_HEADS, HEAD_DIM)


NSA_TQ = 256
NSA_TK = 512
NSA_NC_PAD = 128


def _flash_update(qr, k, v, mask, dist, slope, m_ref, l_ref, a_ref, r):
    s = lax.dot_general(qr, k, (((1,), (1,)), ((), ())), preferred_element_type=F32)
    s = s * (HEAD_DIM ** -0.5) - slope * dist
    s = jnp.where(mask, s, NEG_INF)
    m_old = m_ref[r]
    m_new = jnp.maximum(m_old, jnp.max(s, axis=-1, keepdims=True))
    alpha = jnp.exp(m_old - m_new)
    e = jnp.where(mask, jnp.exp(s - m_new), 0.0)
    l_ref[r] = alpha * l_ref[r] + jnp.sum(e, axis=-1, keepdims=True)
    a_ref[r] = alpha * a_ref[r] + jnp.dot(e.astype(BF16), v, preferred_element_type=F32)
    m_ref[r] = m_new


def _nsa_prompt_kernel(slopes_ref, q_ref, g_ref, kc_ref, vc_ref, ks_ref, vs_ref, kw_ref, vw_ref,
                       ov_ref, ex_ref, o_ref,
                       sel_ref, oc_ref, ms_ref, ls_ref, as_ref, mw_ref, lw_ref, aw_ref, *, n_cmp):
    g = pl.program_id(1)
    qi = pl.program_id(2)
    kt = pl.program_id(3)
    tq = q_ref.shape[0]
    tk = ks_ref.shape[0]
    nsb = ov_ref.shape[1]
    q0 = qi * tq
    last = q0 // tk
    pos_q = q0 + lax.broadcasted_iota(jnp.int32, (tq, 1), 0)

    @pl.when(kt == 0)
    def _():
        for ref in (ms_ref, mw_ref):
            ref[...] = jnp.full(ref.shape, NEG_INF, F32)
        for ref in (ls_ref, lw_ref, as_ref, aw_ref):
            ref[...] = jnp.zeros(ref.shape, F32)
        cidx = lax.broadcasted_iota(jnp.int32, (1, NSA_NC_PAD), 1)
        dc = (pos_q - (cidx * CMP_STRIDE + CMP_BLOCK - 1)).astype(F32)
        mc = jnp.logical_and(dc >= 0, cidx < n_cmp)
        kc = kc_ref[0, 0]
        vc = vc_ref[0, 0]
        ov = ov_ref[...]
        imp = jnp.zeros((tq, nsb), F32)
        for r in range(HEADS_PER_KV):
            qr = q_ref[:, r * HEAD_DIM:(r + 1) * HEAD_DIM]
            s = lax.dot_general(qr, kc, (((1,), (1,)), ((), ())), preferred_element_type=F32)
            s = s * (HEAD_DIM ** -0.5) - slopes_ref[g * HEADS_PER_KV + r] * dc
            s = jnp.where(mc, s, NEG_INF)
            e = jnp.where(mc, jnp.exp(s - jnp.max(s, axis=-1, keepdims=True)), 0.0)
            d = jnp.sum(e, axis=-1, keepdims=True)
            pb = (e / jnp.where(d > 0, d, 1.0)).astype(BF16)
            oc_ref[r] = jnp.dot(pb, vc, preferred_element_type=F32)
            imp = imp + jnp.dot(pb, ov, preferred_element_type=F32)
        jj = lax.broadcasted_iota(jnp.int32, (1, nsb), 1)
        jt = pos_q // SEL_BLOCK
        forced = jnp.logical_or(jj == 0, jnp.logical_or(jj == jt, jj == jt - 1))
        imp = jnp.where(forced, FORCE, jnp.where(jj <= jt, imp, -FORCE))
        sel_ref[...] = jnp.where(_rank_desc(imp, jj) < N_SEL, 1.0, 0.0).astype(BF16)

    @pl.when(kt <= last)
    def _():
        kpos = kt * tk + lax.broadcasted_iota(jnp.int32, (1, tk), 1)
        dist = (pos_q - kpos).astype(F32)
        selk = jnp.dot(sel_ref[...], ex_ref[...], preferred_element_type=F32)
        msel = jnp.logical_and(selk > 0.5, dist >= 0)
        ks = ks_ref[...].astype(BF16)
        vs = vs_ref[...].astype(BF16)
        for r in range(HEADS_PER_KV):
            qr = q_ref[:, r * HEAD_DIM:(r + 1) * HEAD_DIM]
            _flash_update(qr, ks, vs, msel, dist, slopes_ref[g * HEADS_PER_KV + r], ms_ref, ls_ref, as_ref, r)

        @pl.when(kt >= last - 1)
        def _():
            mwin = jnp.logical_and(dist >= 0, dist <= WINDOW)
            kw = kw_ref[...].astype(BF16)
            vw = vw_ref[...].astype(BF16)
            for r in range(HEADS_PER_KV):
                qr = q_ref[:, r * HEAD_DIM:(r + 1) * HEAD_DIM]
                _flash_update(qr, kw, vw, mwin, dist, slopes_ref[g * HEADS_PER_KV + r], mw_ref, lw_ref, aw_ref, r)

    @pl.when(kt == last)
    def _():
        gate = jax.nn.sigmoid(g_ref[...])
        for r in range(HEADS_PER_KV):
            ls = ls_ref[r]
            lw = lw_ref[r]
            o = (gate[:, 3 * r:3 * r + 1] * oc_ref[r]
                 + gate[:, 3 * r + 1:3 * r + 2] * (as_ref[r] / jnp.where(ls > 0, ls, 1.0))
                 + gate[:, 3 * r + 2:3 * r + 3] * (aw_ref[r] / jnp.where(lw > 0, lw, 1.0)))
            o_ref[:, r * HEAD_DIM:(r + 1) * HEAD_DIM] = o.astype(o_ref.dtype)


def _alibi_slopes():
    hh = jnp.arange(1, N_HEADS + 1, dtype=F32)
    return 2.0 ** (-8.0 * hh / N_HEADS)


def _nsa_prompt(q, glog, kv2d, kc, vc, B, T):
    G, R, DH = N_KV_HEADS, HEADS_PER_KV, HEAD_DIM
    tq, tk = NSA_TQ, NSA_TK
    nq, nkt = T // tq, T // tk
    nsb = T // SEL_BLOCK
    n_cmp = T // CMP_STRIDE - CMP_BLOCK // CMP_STRIDE + 1
    ci = jnp.arange(NSA_NC_PAD)[:, None] * CMP_STRIDE
    sj = jnp.arange(nsb)[None, :] * SEL_BLOCK
    overlap = ((ci < sj + SEL_BLOCK) & (ci + CMP_BLOCK > sj) & (jnp.arange(NSA_NC_PAD)[:, None] < n_cmp)).astype(BF16)
    expand = (jnp.arange(nsb)[:, None] == jnp.arange(T)[None, :] // SEL_BLOCK).astype(BF16)

    def last_tile(qi):
        return (qi * tq) // tk

    def sel_rows(b, g, qi, kt):
        return b * nkt + jnp.minimum(kt, last_tile(qi))

    def win_rows(b, g, qi, kt):
        last = last_tile(qi)
        return b * nkt + jnp.clip(kt, jnp.maximum(last - 1, 0), last)

    def kv_spec(rows, branch, kv):
        col0 = (branch * 2 + kv) * G
        return pl.BlockSpec((tk, DH), lambda b, g, qi, kt: (rows(b, g, qi, kt), col0 + g))

    return pl.pallas_call(
        functools.partial(_nsa_prompt_kernel, n_cmp=n_cmp),
        grid=(B, G, nq, nkt),
        in_specs=[
            pl.BlockSpec(memory_space=pltpu.SMEM),
            pl.BlockSpec((tq, R * DH), lambda b, g, qi, kt: (b * nq + qi, g)),
            pl.BlockSpec((None, tq, R * N_BRANCH), lambda b, g, qi, kt: (g, b * nq + qi, 0)),
            pl.BlockSpec((1, 1, NSA_NC_PAD, DH), lambda b, g, qi, kt: (b, g, 0, 0)),
            pl.BlockSpec((1, 1, NSA_NC_PAD, DH), lambda b, g, qi, kt: (b, g, 0, 0)),
            kv_spec(sel_rows, 1, 0), kv_spec(sel_rows, 1, 1),
            kv_spec(win_rows, 2, 0), kv_spec(win_rows, 2, 1),
            pl.BlockSpec((NSA_NC_PAD, nsb), lambda b, g, qi, kt: (0, 0)),
            pl.BlockSpec((nsb, tk), lambda b, g, qi, kt: (0, jnp.minimum(kt, last_tile(qi)))),
        ],
        out_specs=pl.BlockSpec((tq, R * DH), lambda b, g, qi, kt: (b * nq + qi, g)),
        out_shape=jax.ShapeDtypeStruct((B * T, N_HEADS * DH), BF16),
        scratch_shapes=[
            pltpu.VMEM((tq, nsb), BF16),
            pltpu.VMEM((R, tq, DH), F32),
            pltpu.VMEM((R, tq, 1), F32), pltpu.VMEM((R, tq, 1), F32), pltpu.VMEM((R, tq, DH), F32),
            pltpu.VMEM((R, tq, 1), F32), pltpu.VMEM((R, tq, 1), F32), pltpu.VMEM((R, tq, DH), F32),
        ],
        compiler_params=pltpu.CompilerParams(
            dimension_semantics=("parallel", "parallel", "parallel", "arbitrary"),
            vmem_limit_bytes=VMEM_LIMIT_BYTES),
        name="nsa_prompt",
    )(_alibi_slopes(), q, glog, kc, vc, kv2d, kv2d, kv2d, kv2d, overlap, expand)


CMP_PAGES_PER_STEP = 32
SMP_NSB_PAD = 384


def _cmp_sample_kernel(pt_ref, cache_ref, pe_ref, w1_ref, b1_ref, w2_ref, kc_ref, vc_ref, buf, sem, *, n_pages):
    b = pl.program_id(0)
    c = pl.program_id(1)
    pps = CMP_PAGES_PER_STEP
    nseg = pps * (PAGE_SIZE // CMP_STRIDE)

    def page_copy(slot):
        page = pt_ref[b * n_pages + jnp.minimum(c * pps + slot, n_pages - 1)]
        return [pltpu.make_async_copy(cache_ref.at[page, :, cg, :],
                                      buf.at[cg, pl.ds(slot * PAGE_SIZE, PAGE_SIZE), :], sem)
                for cg in range(2 * N_KV_HEADS)]

    for slot in range(pps + 1):
        for cp in page_copy(slot):
            cp.start()
    for slot in range(pps + 1):
        for cp in page_copy(slot):
            cp.wait()
    for kv in range(2):
        out_ref = kc_ref if kv == 0 else vc_ref
        for g in range(N_KV_HEADS):
            cg = kv * N_KV_HEADS + g
            acc = jnp.zeros((nseg, HEAD_DIM), F32)
            for r in range(CMP_BLOCK):
                x = buf[cg, pl.ds(r, nseg, stride=CMP_STRIDE), :] + pe_ref[kv, r:r + 1, :]
                acc = acc + jnp.dot(x.astype(BF16), w1_ref[kv, r], preferred_element_type=F32)
            hid = jax.nn.gelu(acc + b1_ref[kv])
            out = jnp.dot(hid.astype(BF16), w2_ref[kv], preferred_element_type=F32)
            out_ref[0, g] = out.astype(out_ref.dtype)


def _compress_sample(cache_cmp, page_table, pe, w1, b1, w2):
    n_pool = cache_cmp.shape[0]
    B, n_pages = page_table.shape
    G, DH = N_KV_HEADS, HEAD_DIM
    pps = CMP_PAGES_PER_STEP
    nseg = pps * (PAGE_SIZE // CMP_STRIDE)
    cache2 = cache_cmp.reshape(n_pool, PAGE_SIZE, 2 * G, DH)
    w1b = w1.reshape(2, CMP_BLOCK, DH, DH).astype(BF16)
    out_sds = jax.ShapeDtypeStruct((B, G, n_pages * (PAGE_SIZE // CMP_STRIDE), DH), BF16)
    out_spec = pl.BlockSpec((1, G, nseg, DH), lambda b, c, pt: (b, 0, c, 0))
    full = lambda shape: pl.BlockSpec(shape, lambda b, c, pt: (0,) * len(shape))
    return pl.pallas_call(
        functools.partial(_cmp_sample_kernel, n_pages=n_pages),
        grid_spec=pltpu.PrefetchScalarGridSpec(
            num_scalar_prefetch=1,
            grid=(B, n_pages // pps),
            in_specs=[
                pl.BlockSpec(memory_space=pl.ANY),
                full((2, CMP_BLOCK, DH)), full((2, CMP_BLOCK, DH, DH)), full((2, 1, DH)), full((2, DH, DH)),
            ],
            out_specs=[out_spec, out_spec],
            scratch_shapes=[pltpu.VMEM((2 * G, (pps + 1) * PAGE_SIZE, DH), F32), pltpu.SemaphoreType.DMA(())],
        ),
        out_shape=[out_sds, out_sds],
        compiler_params=pltpu.CompilerParams(
            dimension_semantics=("arbitrary", "arbitrary"), vmem_limit_bytes=VMEM_LIMIT_BYTES),
        name="cmp_sample",
    )(page_table.reshape(-1).astype(jnp.int32), cache2, pe, w1b, b1.reshape(2, 1, DH), w2.astype(BF16))


def _smp_select_kernel(q_ref, slope_ref, kc_ref, vc_ref, ov_ref, oc_ref, top_ref, *, n_cmp, pos):
    q = q_ref[0].astype(BF16)
    ncp = kc_ref.shape[2]
    nsbp = ov_ref.shape[1]
    cidx = lax.broadcasted_iota(jnp.int32, (1, ncp), 1)
    dc = (pos - (cidx * CMP_STRIDE + CMP_BLOCK - 1)).astype(F32)
    mc = jnp.logical_and(dc >= 0, cidx < n_cmp)
    s = lax.dot_general(q, kc_ref[0, 0], (((1,), (1,)), ((), ())), preferred_element_type=F32)
    s = s * (HEAD_DIM ** -0.5) - slope_ref[0] * dc
    s = jnp.where(mc, s, NEG_INF)
    e = jnp.where(mc, jnp.exp(s - jnp.max(s, axis=-1, keepdims=True)), 0.0)
    d = jnp.sum(e, axis=-1, keepdims=True)
    pb = (e / jnp.where(d > 0, d, 1.0)).astype(BF16)
    oc_ref[0] = jnp.dot(pb, vc_ref[0, 0], preferred_element_type=F32)
    imp = jnp.sum(jnp.dot(pb, ov_ref[...], preferred_element_type=F32), axis=0, keepdims=True)
    jj = lax.broadcasted_iota(jnp.int32, (1, nsbp), 1)
    jj_f = jj.astype(F32)
    jt = pos // SEL_BLOCK
    forced = jnp.logical_or(jj == 0, jnp.logical_or(jj == jt, jj == jt - 1))
    imp = jnp.where(forced, FORCE, jnp.where(jj <= jt, imp, -FORCE))
    lane = lax.broadcasted_iota(jnp.int32, (1, LANES), 1)
    top = jnp.zeros((1, LANES), F32)
    for k in range(N_SEL):
        best = jnp.max(imp, axis=-1, keepdims=True)
        idx = jnp.min(jnp.where(imp == best, jj_f, 1e9), axis=-1, keepdims=True)
        top = jnp.where(lane == k, idx, top)
        imp = jnp.where(jj_f == idx, -3e38, imp)
    top_ref[0, 0] = top.astype(jnp.int32)


def _smp_attend_kernel(top_ref, pt_ref, q_ref, slope_ref, glog_ref, oc_ref, ks_ref, vs_ref, ksn_ref, vsn_ref,
                       kw_ref, vw_ref, kwn_ref, vwn_ref, o_ref, m_ref, l_ref, a_ref, ow_ref, *, pos):
    b = pl.program_id(0)
    g = pl.program_id(1)
    k = pl.program_id(2)
    scale = HEAD_DIM ** -0.5
    q = q_ref[0].astype(BF16)
    qf = q.astype(F32)
    slope = slope_ref[0]
    jt = pos // SEL_BLOCK

    def new_row(ref):
        return ref[pl.ds(b, 1), :].astype(BF16).astype(F32)

    @pl.when(k == 0)
    def _():
        nw = kw_ref.shape[1]
        dist = (nw - lax.broadcasted_iota(jnp.int32, (1, nw), 1)).astype(F32)
        s = lax.dot_general(q, kw_ref[0].astype(BF16), (((1,), (1,)), ((), ())), preferred_element_type=F32)
        s = s * scale - slope * dist
        s_new = jnp.sum(qf * new_row(kwn_ref), axis=-1, keepdims=True) * scale
        m = jnp.maximum(jnp.max(s, axis=-1, keepdims=True), s_new)
        e = jnp.exp(s - m)
        e_new = jnp.exp(s_new - m)
        d = jnp.sum(e, axis=-1, keepdims=True) + e_new
        ow = jnp.dot((e / d).astype(BF16), vw_ref[0].astype(BF16), preferred_element_type=F32)
        ow_ref[...] = ow + (e_new / d).astype(BF16).astype(F32) * new_row(vwn_ref)
        m_ref[...] = jnp.sum(qf * new_row(ksn_ref), axis=-1, keepdims=True) * scale
        l_ref[...] = jnp.ones(l_ref.shape, F32)
        a_ref[...] = jnp.broadcast_to(new_row(vsn_ref), a_ref.shape)

    top = top_ref[(b * N_KV_HEADS + g) * LANES + k]

    @pl.when(top != jt)
    def _():
        kpos = top * SEL_BLOCK + lax.broadcasted_iota(jnp.int32, (1, SEL_BLOCK), 1)
        dist = (pos - kpos).astype(F32)
        s = lax.dot_general(q, ks_ref[0].astype(BF16), (((1,), (1,)), ((), ())), preferred_element_type=F32)
        s = s * scale - slope * dist
        m_old = m_ref[...]
        m_new = jnp.maximum(m_old, jnp.max(s, axis=-1, keepdims=True))
        alpha = jnp.exp(m_old - m_new)
        e = jnp.exp(s - m_new)
        l_ref[...] = alpha * l_ref[...] + jnp.sum(e, axis=-1, keepdims=True)
        a_ref[...] = alpha * a_ref[...] + jnp.dot(e.astype(BF16), vs_ref[0].astype(BF16), preferred_element_type=F32)
        m_ref[...] = m_new

    @pl.when(k == N_SEL - 1)
    def _():
        gate = jax.nn.sigmoid(glog_ref[0, 0])
        o_ref[0] = (gate[:, 0:1] * oc_ref[0] + gate[:, 1:2] * (a_ref[...] / l_ref[...])
                    + gate[:, 2:3] * ow_ref[...])


def _nsa_sample(q, glog, kv2d, row0, kc, vc, cache_slc, state_win, page_table):
    B, n_pages = page_table.shape
    G, R, DH = N_KV_HEADS, HEADS_PER_KV, HEAD_DIM
    pos = n_pages * PAGE_SIZE
    ncp = kc.shape[2]
    n_cmp = (pos + 1) // CMP_STRIDE - CMP_BLOCK // CMP_STRIDE + 1
    nsb = _cdiv(pos + 1, SEL_BLOCK)
    assert nsb <= SMP_NSB_PAD and n_cmp <= ncp and row0 % 8 == 0 and B <= 8
    slopes = _alibi_slopes().reshape(G, R, 1)
    ci = jnp.arange(ncp)[:, None] * CMP_STRIDE
    sj = jnp.arange(SMP_NSB_PAD)[None, :] * SEL_BLOCK
    overlap = ((ci < sj + SEL_BLOCK) & (ci + CMP_BLOCK > sj) & (jnp.arange(ncp)[:, None] < n_cmp)).astype(BF16)
    head_spec = lambda *_: None
    oc, top = pl.pallas_call(
        functools.partial(_smp_select_kernel, n_cmp=n_cmp, pos=pos),
        grid=(B, G),
        in_specs=[
            pl.BlockSpec((1, R, DH), lambda b, g: (b, g, 0)),
            pl.BlockSpec((1, R, 1), lambda b, g: (g, 0, 0)),
            pl.BlockSpec((1, 1, ncp, DH), lambda b, g: (b, g, 0, 0)),
            pl.BlockSpec((1, 1, ncp, DH), lambda b, g: (b, g, 0, 0)),
            pl.BlockSpec((ncp, SMP_NSB_PAD), lambda b, g: (0, 0)),
        ],
        out_specs=[pl.BlockSpec((1, R, DH), lambda b, g: (b, g, 0)),
                   pl.BlockSpec((1, 1, 1, LANES), lambda b, g: (b, g, 0, 0))],
        out_shape=[jax.ShapeDtypeStruct((B, G * R, DH), F32), jax.ShapeDtypeStruct((B, G, 1, LANES), jnp.int32)],
        compiler_params=pltpu.CompilerParams(
            dimension_semantics=("parallel", "parallel"), vmem_limit_bytes=VMEM_LIMIT_BYTES),
        name="smp_select",
    )(q, slopes, kc, vc, overlap)

    n_pool = cache_slc.shape[0]
    halves = PAGE_SIZE // SEL_BLOCK
    slc2 = cache_slc.reshape(n_pool * halves, SEL_BLOCK, 2 * G * DH)
    win2 = state_win.reshape(B, state_win.shape[1], 2 * G * DH)
    jt = pos // SEL_BLOCK

    def blk(b, g, k, top_ref, pt_ref):
        j = jnp.minimum(top_ref[(b * G + g) * LANES + k], jt - 1)
        return pt_ref[b * n_pages + j // halves] * halves + j % halves

    def new_spec(branch, kv):
        col0 = (branch * 2 + kv) * G
        return pl.BlockSpec((8, DH), lambda b, g, k, t, p: (row0 // 8, col0 + g))

    return pl.pallas_call(
        functools.partial(_smp_attend_kernel, pos=pos),
        grid_spec=pltpu.PrefetchScalarGridSpec(
            num_scalar_prefetch=2,
            grid=(B, G, N_SEL),
            in_specs=[
                pl.BlockSpec((1, R, DH), lambda b, g, k, t, p: (b, g, 0)),
                pl.BlockSpec((1, R, 1), lambda b, g, k, t, p: (g, 0, 0)),
                pl.BlockSpec((1, 1, R, N_BRANCH), lambda b, g, k, t, p: (b, g, 0, 0)),
                pl.BlockSpec((1, R, DH), lambda b, g, k, t, p: (b, g, 0)),
                pl.BlockSpec((1, SEL_BLOCK, DH), lambda b, g, k, t, p: (blk(b, g, k, t, p), 0, g)),
                pl.BlockSpec((1, SEL_BLOCK, DH), lambda b, g, k, t, p: (blk(b, g, k, t, p), 0, G + g)),
                new_spec(1, 0), new_spec(1, 1),
                pl.BlockSpec((1, win2.shape[1], DH), lambda b, g, k, t, p: (b, 0, g)),
                pl.BlockSpec((1, win2.shape[1], DH), lambda b, g, k, t, p: (b, 0, G + g)),
                new_spec(2, 0), new_spec(2, 1),
            ],
            out_specs=pl.BlockSpec((1, R, DH), lambda b, g, k, t, p: (b, g, 0)),
            scratch_shapes=[pltpu.VMEM((R, 1), F32), pltpu.VMEM((R, 1), F32), pltpu.VMEM((R, DH), F32),
                            pltpu.VMEM((R, DH), F32)],
        ),
        out_shape=jax.ShapeDtypeStruct((B, G * R, DH), F32),
        compiler_params=pltpu.CompilerParams(
            dimension_semantics=("arbitrary", "arbitrary", "arbitrary"), vmem_limit_bytes=VMEM_LIMIT_BYTES),
        name="smp_attend",
    )(top.reshape(-1), page_table.reshape(-1).astype(jnp.int32), q, slopes, glog, oc, slc2, slc2, kv2d, kv2d,
      win2, win2, kv2d, kv2d)


def _nsa_mixer(hn, kv2d, p, j, n_prompt_seq, seq_len, cache_cmp, cache_slc, state_win, page_table):
    N, D = hn.shape
    HD = N_HEADS * HEAD_DIM
    G, R = N_KV_HEADS, HEADS_PER_KV
    B, T = n_prompt_seq, seq_len
    Np = B * T
    Bs = N - Np
    w_qg = p['w_qg'][j]
    q = _mm(hn, w_qg[None, :, :HD], 0, out_dtype=BF16)
    n_gate = N_BRANCH * N_HEADS
    glog = _mm(hn, jnp.pad(w_qg[None, :, HD:], ((0, 0), (0, 0), (0, LANES - n_gate))), 0)[:, :n_gate]
    kv_p = kv2d[:Np].reshape(B, T, N_BRANCH, 2, G, HEAD_DIM)
    kc, vc = (_compress(kv_p[:, :, 0, i], p['pe_cmp'][i], p['w_cmp1'][i], p['b_cmp1'][i], p['w_cmp2'][i])
              for i in range(2))
    pad = NSA_NC_PAD - kc.shape[1]
    kcp = jnp.pad(kc, ((0, 0), (0, pad), (0, 0), (0, 0))).transpose(0, 2, 1, 3).astype(BF16)
    vcp = jnp.pad(vc, ((0, 0), (0, pad), (0, 0), (0, 0))).transpose(0, 2, 1, 3).astype(BF16)
    o_p = _nsa_prompt(q, glog[:Np].reshape(Np, G, R * N_BRANCH).transpose(1, 0, 2), kv2d, kcp, vcp, B, T)
    kcs, vcs = _compress_sample(cache_cmp, page_table, p['pe_cmp'], p['w_cmp1'], p['b_cmp1'], p['w_cmp2'])
    o_s = _nsa_sample(q[Np:].astype(F32).reshape(Bs, N_HEADS, HEAD_DIM), glog[Np:].reshape(Bs, G, R, N_BRANCH),
                      kv2d, Np, kcs, vcs, cache_slc, state_win, page_table)
    o = jnp.concatenate([o_p, o_s.reshape(Bs, HD).astype(BF16)], axis=0)
    return _mm(o, p['w_o'], j)


def kernel(x_prompt, x_sample, c_prompt, c_sample, cache_cmp_kv, cache_slc_kv, state_win_kv, page_table,
           w_ada, b_ada, g_norm_mix, g_norm_ffn, w_gmlp_in, b_gmlp_in, g_sgu_ln, b_sgu_ln, w_sgu, b_sgu,
           w_gmlp_out, w_ada_kv, b_ada_kv, g_norm_kv, w_kv, pe_cmp, w_cmp1, b_cmp1, w_cmp2, w_qg, w_o,
           w_router, b_router, w_exp1, w_exp3, w_exp2, w_sh1, w_sh3, w_sh2, g_final):
    p = dict(w_gmlp_in=w_gmlp_in, b_gmlp_in=b_gmlp_in, g_sgu_ln=g_sgu_ln, b_sgu_ln=b_sgu_ln,
             w_sgu=w_sgu, b_sgu=b_sgu, w_gmlp_out=w_gmlp_out, pe_cmp=pe_cmp, w_cmp1=w_cmp1, b_cmp1=b_cmp1,
             w_cmp2=w_cmp2, w_qg=w_qg, w_o=w_o, w_router=w_router, b_router=b_router, w_exp1=w_exp1,
             w_exp3=w_exp3, w_exp2=w_exp2, w_sh1=w_sh1, w_sh3=w_sh3, w_sh2=w_sh2)
    B, T, D = x_prompt.shape
    Bs = x_sample.shape[0]
    Np = B * T
    assert x_sample.shape[1] == 1 and T % ROW_TILE == 0 and Bs <= ROW_TILE
    h = jnp.concatenate([x_prompt.reshape(Np, D), x_sample.reshape(Bs, D)], axis=0)
    c_act = jax.nn.silu(jnp.concatenate([c_prompt, c_sample], axis=0))
    resnorm = functools.partial(_resnorm, n_prompt_seq=B, seq_len=T)
    zeros_tab = jnp.zeros((B + Bs, D), F32)
    v_rows, kv2d = None, None
    pending, gate = [], None
    for l in range(DEPTH):
        mod = _mm(c_act, w_ada, l, bias=b_ada[l])
        sh1, sc1, gt1, sh2, sc2, gt2 = jnp.split(mod, 6, axis=-1)
        if pending:
            h, hn = resnorm(h, pending, gate, g_norm_mix[l], sc1, sh1)
        else:
            hn = resnorm(h, [], None, g_norm_mix[l], sc1, sh1)
        if l < N_A_LAYERS:
            mix, v_rows = _gmlp_mixer(hn, p, l, B, T)
        else:
            if kv2d is None:
                mkv = _mm(c_act, w_ada_kv[None], 0, bias=b_ada_kv)
                sh_kv, sc_kv = jnp.split(mkv, 2, axis=-1)
                kvn = resnorm(h, [], None, g_norm_kv, sc_kv, sh_kv)
                kv2d = _mm(kvn, w_kv[None], 0)
            mix = _nsa_mixer(hn, kv2d, p, l - N_A_LAYERS, B, T, cache_cmp_kv, cache_slc_kv, state_win_kv, page_table)
        h, hn, logits = resnorm(h, [mix], gt1, g_norm_ffn[l], sc2, sh2, w_router=w_router[l])
        pending, gate = list(_moe_ffn(hn, logits, p, l)), gt2
    y = resnorm(h, pending, gate, g_final, zeros_tab, zeros_tab, out_dtype=F32, write_h=False)
    G, DH = N_KV_HEADS, HEAD_DIM
    y_prompt = y[:Np].reshape(B, T, D)
    y_sample = y[Np:].reshape(Bs, 1, D)
    v_p = v_rows[:Np].reshape(B, T, GMLP_DFF)[:, T - CHUNK:][None]
    v_s = v_rows[Np:].reshape(Bs, 1, GMLP_DFF)[None]
    kv_p = kv2d[:Np].reshape(B, T, N_BRANCH, 2, G, DH)
    kv_s = kv2d[Np:].reshape(Bs, 1, N_BRANCH, 2, G, DH)
    n_pg = T // PAGE_SIZE
    cmp_kv_prompt = kv_p[:, :, 0].reshape(B, n_pg, PAGE_SIZE, 2, G, DH)
    slc_kv_prompt = kv_p[:, :, 1].reshape(B, n_pg, PAGE_SIZE, 2, G, DH)
    win_kv_prompt = kv_p[:, T - min(WINDOW, T):, 2]
    return (y_prompt, y_sample, v_p, v_s, cmp_kv_prompt, kv_s[:, :, 0], slc_kv_prompt, kv_s[:, :, 1],
            win_kv_prompt, kv_s[:, :, 2])
```
